```python
import jax, jax.numpy as jnp
from jax import lax
import numpy as np

D_MODEL = 1024
BATCH = 8
SEQ = 2048
DEPTH = 2

N_A_LAYERS = DEPTH // 2
N_B_LAYERS = DEPTH - N_A_LAYERS
POOL_WINDOWS = (2, 4, 8, 16)
N_POOL_GROUPS = len(POOL_WINDOWS)
POOL_GROUP = D_MODEL // N_POOL_GROUPS
HEAD_DIM = 64
N_HEADS = D_MODEL // HEAD_DIM
N_KV_HEADS = 4
GQA_GROUP = N_HEADS // N_KV_HEADS
WINDOW = 128
BLOCK = 128
D_FF = -(-8 * D_MODEL // (3 * 256)) * 256
PLE_DIM = 256
EPS = 1e-6
NEG_INF = -1e30

kernel_name = "yoco_pool_swa_sink_hybrid"


def rmsnorm(x, g):
    xf = x.astype(jnp.float32)
    y = xf * lax.rsqrt(jnp.mean(xf * xf, axis=-1, keepdims=True) + EPS)
    return (y * g.astype(jnp.float32)).astype(x.dtype)


def causal_multiscale_pool(h):
    B, S, C = h.shape
    hf = h.astype(jnp.float32)
    cp = jnp.concatenate([jnp.zeros((B, 1, C), jnp.float32), jnp.cumsum(hf, axis=1)], axis=1)
    pos1 = jnp.arange(1, S + 1, dtype=jnp.int32)
    outs = []
    for gi, w in enumerate(POOL_WINDOWS):
        sl = cp[:, :, gi * POOL_GROUP:(gi + 1) * POOL_GROUP]
        hi = sl[:, 1:]
        lo = jnp.concatenate([jnp.zeros((B, w - 1, POOL_GROUP), jnp.float32),
                              sl[:, :S - w + 1]], axis=1)
        cnt = jnp.minimum(pos1, w).astype(jnp.float32)[None, :, None]
        outs.append((hi - lo) / cnt)
    pooled = jnp.concatenate(outs, axis=-1)
    return (pooled - hf).astype(h.dtype)


def pool_mixer(h, w_pool, scale):
    B, S, _ = h.shape
    d = causal_multiscale_pool(h).reshape(B, S, N_POOL_GROUPS, POOL_GROUP)
    y = jnp.einsum('bsgc,gcd->bsgd', d, w_pool).reshape(B, S, D_MODEL)
    return y * scale


def alibi_slopes():
    h = jnp.arange(1, N_HEADS + 1, dtype=jnp.float32)
    return jnp.exp2(-8.0 * h / N_HEADS).reshape(N_KV_HEADS, GQA_GROUP)


def swa_sink_attention(q, k, v, sinks):
    B, S = q.shape[:2]
    nb = S // BLOCK
    qb = q.reshape(B, nb, BLOCK, N_KV_HEADS, GQA_GROUP, HEAD_DIM)
    kb = k.reshape(B, nb, BLOCK, N_KV_HEADS, HEAD_DIM)
    vb = v.reshape(B, nb, BLOCK, N_KV_HEADS, HEAD_DIM)
    pad = ((0, 0), (1, 0), (0, 0), (0, 0), (0, 0))
    kwin = jnp.concatenate([jnp.pad(kb, pad)[:, :-1], kb], axis=2)
    vwin = jnp.concatenate([jnp.pad(vb, pad)[:, :-1], vb], axis=2)
    scores = jnp.einsum('bnqkgd,bnskd->bnkgqs', qb, kwin,
                        preferred_element_type=jnp.float32) * (HEAD_DIM ** -0.5)
    qi = jnp.arange(BLOCK)[:, None]
    si = jnp.arange(2 * BLOCK)[None, :]
    rel = BLOCK + qi - si
    kpos = (jnp.arange(nb)[:, None, None] - 1) * BLOCK + si[None]
    valid = (rel >= 0)[None] & (rel < WINDOW)[None] & (kpos >= 0)
    bias = -alibi_slopes()[:, :, None, None] * rel.astype(jnp.float32)
    scores = jnp.where(valid[None, :, None, None], scores + bias[None, None], NEG_INF)
    sink = jnp.broadcast_to(sinks.astype(jnp.float32).reshape(N_KV_HEADS, GQA_GROUP)[None, None, :, :, None, None],
                            scores.shape[:-1] + (1,))
    probs = jax.nn.softmax(jnp.concatenate([scores, sink], axis=-1), axis=-1)[..., :-1]
    out = jnp.einsum('bnkgqs,bnskd->bnqkgd', probs.astype(v.dtype), vwin)
    return out.reshape(B, S, N_HEADS * HEAD_DIM)


def swiglu(h, w_gu, w_down):
    gu = h @ w_gu
    g, u = gu[..., :D_FF], gu[..., D_FF:]
    return (jax.nn.silu(g) * u) @ w_down


def setup_inputs(seed: int = 0) -> dict:
    key = jax.random.key(seed)
    ks = jax.random.split(key, 24)
    f32 = jnp.float32
    nrm = lambda k, s, fan: jax.random.normal(k, s, f32) * (fan ** -0.5)
    gain = lambda k, s: 1.0 + 0.1 * jax.random.normal(k, s, f32)
    KV = N_KV_HEADS * HEAD_DIM
    QD = N_HEADS * HEAD_DIM
    return {
        "x": jax.random.normal(ks[0], (BATCH, SEQ, D_MODEL), f32),
        "p": jax.random.normal(ks[1], (DEPTH, BATCH, SEQ, PLE_DIM), f32),
        "pre_mix_g": gain(ks[2], (DEPTH, D_MODEL)),
        "post_mix_g": gain(ks[3], (DEPTH, D_MODEL)),
        "pre_ffn_g": gain(ks[4], (DEPTH, D_MODEL)),
        "post_ffn_g": gain(ks[5], (DEPTH, D_MODEL)),
        "pool_w": nrm(ks[6], (N_A_LAYERS, N_POOL_GROUPS, POOL_GROUP, POOL_GROUP), POOL_GROUP),
        "pool_scale": gain(ks[7], (N_A_LAYERS, D_MODEL)),
        "kv_g": gain(ks[8], (D_MODEL,)),
        "w_kv": nrm(ks[9], (D_MODEL, 2 * KV), D_MODEL),
        "w_q": nrm(ks[10], (N_B_LAYERS, D_MODEL, QD), D_MODEL),
        "sinks": 0.5 * jax.random.normal(ks[11], (N_B_LAYERS, N_HEADS), f32),
        "w_o": nrm(ks[12], (N_B_LAYERS, QD, D_MODEL), QD),
        "w_gu": nrm(ks[13], (DEPTH, D_MODEL, 2 * D_FF), D_MODEL),
        "w_down": nrm(ks[14], (DEPTH, D_FF, D_MODEL), D_FF),
        "ple_g": gain(ks[15], (DEPTH, D_MODEL)),
        "w_ple_gate": nrm(ks[16], (DEPTH, D_MODEL, D_MODEL), D_MODEL),
        "w_ple_proj": nrm(ks[17], (DEPTH, PLE_DIM, D_MODEL), PLE_DIM),
        "ple_post_g": gain(ks[18], (DEPTH, D_MODEL)),
    }


def reference(x, p, pre_mix_g, post_mix_g, pre_ffn_g, post_ffn_g, pool_w, pool_scale,
              kv_g, w_kv, w_q, sinks, w_o, w_gu, w_down, ple_g, w_ple_gate, w_ple_proj,
              ple_post_g):
    B, S, _ = x.shape
    KV = N_KV_HEADS * HEAD_DIM
    k = v = None
    for i in range(DEPTH):
        h = rmsnorm(x, pre_mix_g[i])
        if i < N_A_LAYERS:
            y = pool_mixer(h, pool_w[i], pool_scale[i])
        else:
            b = i - N_A_LAYERS
            q = (h @ w_q[b]).reshape(B, S, N_HEADS, HEAD_DIM)
            y = swa_sink_attention(q, k, v, sinks[b]) @ w_o[b]
        x = x + rmsnorm(y, post_mix_g[i])
        h = rmsnorm(x, pre_ffn_g[i])
        x = x + rmsnorm(swiglu(h, w_gu[i], w_down[i]), post_ffn_g[i])
        gate = jax.nn.sigmoid(rmsnorm(x, ple_g[i]) @ w_ple_gate[i])
        e = (p[i].astype(x.dtype) @ w_ple_proj[i]) * gate
        x = x + rmsnorm(e, ple_post_g[i])
        if i == N_A_LAYERS - 1:
            kv = rmsnorm(x, kv_g) @ w_kv
            k = kv[..., :KV].reshape(B, S, N_KV_HEADS, HEAD_DIM)
            v = kv[..., KV:].reshape(B, S, N_KV_HEADS, HEAD_DIM)
    return x
```

```python
import functools

import jax
import jax.numpy as jnp
from jax import lax
from jax.experimental import pallas as pl
from jax.experimental.pallas import tpu as pltpu

D_MODEL = 1024
POOL_WINDOWS = (2, 4, 8, 16)
POOL_GROUP = D_MODEL // len(POOL_WINDOWS)
POOL_HALO = 16
HEAD_DIM = 64
N_HEADS = D_MODEL // HEAD_DIM
N_KV_HEADS = 4
GQA_GROUP = N_HEADS // N_KV_HEADS
ATTN_BLOCK = 128
D_FF = 2816
PLE_DIM = 256
EPS = 1e-6
NEG_INF = -1e30
LANES = 128
KV_DUP = N_KV_HEADS * LANES

TILE_ROWS = 256
FF_CHUNK = 256
VMEM_LIMIT_BYTES = 56 * 1024 * 1024

BF16 = jnp.bfloat16
F32 = jnp.float32

G_PRE_MIX, G_POST_MIX, G_PRE_FFN, G_POST_FFN, G_PLE, G_PLE_POST, G_POOL_SCALE, G_KV = range(8)


def _rms(x, g):
    return x * lax.rsqrt(jnp.mean(x * x, axis=-1, keepdims=True) + EPS) * g


def _sigmoid(x):
    return 1.0 / (1.0 + jnp.exp(-x))


def _mm(a, b):
    return jnp.dot(a, b, preferred_element_type=F32)


def _ffn_ple(x1, p_ref, gain, wgu_ref, wdown_ref, wgate_ref, wproj_ref, act_ref):
    h = _rms(x1, gain(G_PRE_FFN)).astype(BF16)
    for c in range(D_FF // FF_CHUNK):
        lo = c * FF_CHUNK
        gate = _mm(h, wgu_ref[:, lo:lo + FF_CHUNK])
        up = _mm(h, wgu_ref[:, D_FF + lo:D_FF + lo + FF_CHUNK])
        act_ref[:, lo:lo + FF_CHUNK] = (gate * _sigmoid(gate) * up).astype(BF16)
    x2 = x1 + _rms(_mm(act_ref[...], wdown_ref[...]), gain(G_POST_FFN))
    ple_gate = _sigmoid(_mm(_rms(x2, gain(G_PLE)).astype(BF16), wgate_ref[...]))
    e = _mm(p_ref[...].astype(BF16), wproj_ref[...]) * ple_gate
    return x2 + _rms(e, gain(G_PLE_POST))


def _layer0_body(x_ref, xhalo_ref, p_ref, gains_ref, poolw_ref, wgu_ref, wdown_ref, wgate_ref,
                 wproj_ref, wkv_ref, xo_ref, k_ref, v_ref, hext_ref, act_ref):
    tile = pl.program_id(1)
    tm = x_ref.shape[0]
    gains = gains_ref[...]
    gain = lambda r: gains[r:r + 1, :]

    x = x_ref[...]
    h = _rms(x, gain(G_PRE_MIX))
    h_halo = _rms(xhalo_ref[...], gain(G_PRE_MIX))
    hext_ref[0:POOL_HALO, :] = jnp.where(tile == 0, 0.0, h_halo)
    hext_ref[POOL_HALO:, :] = h

    pos1 = lax.broadcasted_iota(jnp.int32, (tm, POOL_GROUP), 0) + (tile * tm + 1)
    mixed = []
    for gi, w in enumerate(POOL_WINDOWS):
        lo = gi * POOL_GROUP
        terms = [hext_ref[POOL_HALO - j:POOL_HALO - j + tm, lo:lo + POOL_GROUP] for j in range(w)]
        while len(terms) > 1:
            terms = [terms[a] + terms[a + 1] for a in range(0, len(terms), 2)]
        cnt = jnp.minimum(pos1, w).astype(F32)
        d = terms[0] / cnt - h[:, lo:lo + POOL_GROUP]
        mixed.append(_mm(d.astype(BF16), poolw_ref[gi]))
    y = jnp.concatenate(mixed, axis=-1) * gain(G_POOL_SCALE)
    x1 = x + _rms(y, gain(G_POST_MIX))

    x3 = _ffn_ple(x1, p_ref, gain, wgu_ref, wdown_ref, wgate_ref, wproj_ref, act_ref)
    xo_ref[...] = x3

    kv = _mm(_rms(x3, gain(G_KV)).astype(BF16), wkv_ref[...])
    k_ref[...] = kv[:, :KV_DUP].astype(BF16)
    v_ref[...] = kv[:, KV_DUP:].astype(BF16)


def _layer1_body(x_ref, k_ref, khalo_ref, v_ref, vhalo_ref, p_ref, gains_ref, bias_ref, sink_ref,
                 wq_ref, wo_ref, wgu_ref, wdown_ref, wgate_ref, wproj_ref, xo_ref, attn_ref, act_ref):
    tile = pl.program_id(1)
    tm = x_ref.shape[0]
    gains = gains_ref[...]
    gain = lambda r: gains[r:r + 1, :]

    x = x_ref[...]
    q = _mm(_rms(x, gain(G_PRE_MIX)).astype(BF16), wq_ref[...]) * (HEAD_DIM ** -0.5)

    left = lax.broadcasted_iota(jnp.int32, (ATTN_BLOCK, LANES), 1) < HEAD_DIM
    no_prev = jnp.where(tile == 0, NEG_INF, 0.0)
    for j in range(tm // ATTN_BLOCK):
        r0 = j * ATTN_BLOCK
        for kh in range(N_KV_HEADS):
            c0 = kh * LANES
            if j == 0:
                k_prev, v_prev = khalo_ref[:, c0:c0 + LANES], vhalo_ref[:, c0:c0 + LANES]
            else:
                k_prev = k_ref[r0 - ATTN_BLOCK:r0, c0:c0 + LANES]
                v_prev = v_ref[r0 - ATTN_BLOCK:r0, c0:c0 + LANES]
            k_win = jnp.concatenate([k_prev, k_ref[r0:r0 + ATTN_BLOCK, c0:c0 + LANES]], axis=0)
            v_win = jnp.concatenate([v_prev, v_ref[r0:r0 + ATTN_BLOCK, c0:c0 + LANES]], axis=0)
            q_heads = []
            for g in range(GQA_GROUP):
                pair = (kh * GQA_GROUP + g) // 2
                q_pair = q[r0:r0 + ATTN_BLOCK, pair * LANES:(pair + 1) * LANES]
                q_heads.append(jnp.where(left if g % 2 == 0 else ~left, q_pair, 0.0))
            q4 = jnp.concatenate(q_heads, axis=0).astype(BF16)
            s = lax.dot_general(q4, k_win, (((1,), (1,)), ((), ())), preferred_element_type=F32)
            s = s + bias_ref[kh]
            s_prev, s_cur = s[:, :ATTN_BLOCK], s[:, ATTN_BLOCK:]
            if j == 0:
                s_prev = s_prev + no_prev
            sink = sink_ref[kh]
            m = jnp.max(jnp.maximum(jnp.maximum(s_prev, s_cur), sink), axis=-1, keepdims=True)
            p_prev, p_cur = jnp.exp(s_prev - m), jnp.exp(s_cur - m)
            denom = jnp.sum(p_prev + p_cur, axis=-1, keepdims=True) + jnp.exp(sink - m)
            probs = jnp.concatenate([p_prev, p_cur], axis=-1).astype(BF16)
            o = _mm(probs, v_win) / denom
            for half in range(GQA_GROUP // 2):
                a = o[(2 * half) * ATTN_BLOCK:(2 * half + 1) * ATTN_BLOCK]
                b = o[(2 * half + 1) * ATTN_BLOCK:(2 * half + 2) * ATTN_BLOCK]
                pair = kh * (GQA_GROUP // 2) + half
                attn_ref[r0:r0 + ATTN_BLOCK, pair * LANES:(pair + 1) * LANES] = (
                    jnp.where(left, a, b).astype(BF16))

    x1 = x + _rms(_mm(attn_ref[...], wo_ref[...]), gain(G_POST_MIX))
    xo_ref[...] = _ffn_ple(x1, p_ref, gain, wgu_ref, wdown_ref, wgate_ref, wproj_ref, act_ref)


def _resident(shape):
    return pl.BlockSpec(shape, lambda b, i: (0,) * len(shape), pipeline_mode=pl.Buffered(1))


def _dup_heads(w):
    d = w.shape[0]
    w = w.reshape(d, N_KV_HEADS, 1, HEAD_DIM)
    return jnp.broadcast_to(w, (d, N_KV_HEADS, LANES // HEAD_DIM, HEAD_DIM)).reshape(d, KV_DUP)


def _attention_tables(sinks):
    qi = jnp.arange(ATTN_BLOCK)[:, None]
    si = jnp.arange(2 * ATTN_BLOCK)[None, :]
    rel = ATTN_BLOCK + qi - si
    valid = (rel >= 0) & (rel < ATTN_BLOCK)
    heads = jnp.arange(1, N_HEADS + 1, dtype=F32)
    slopes = jnp.exp2(-8.0 * heads / N_HEADS).reshape(N_KV_HEADS, GQA_GROUP)
    bias = jnp.where(valid[None, None], -slopes[:, :, None, None] * rel.astype(F32)[None, None], NEG_INF)
    bias = bias.reshape(N_KV_HEADS, GQA_GROUP * ATTN_BLOCK, 2 * ATTN_BLOCK)
    sink = sinks.astype(F32).reshape(N_KV_HEADS, GQA_GROUP, 1, 1)
    sink = jnp.broadcast_to(sink, (N_KV_HEADS, GQA_GROUP, ATTN_BLOCK, LANES))
    return bias, sink.reshape(N_KV_HEADS, GQA_GROUP * ATTN_BLOCK, LANES)


def kernel(x, p, pre_mix_g, post_mix_g, pre_ffn_g, post_ffn_g, pool_w, pool_scale, kv_g, w_kv, w_q, sinks,
           w_o, w_gu, w_down, ple_g, w_ple_gate, w_ple_proj, ple_post_g):
    batch, seq, _ = x.shape
    tm = TILE_ROWS
    assert seq % tm == 0 and tm % ATTN_BLOCK == 0 and tm % POOL_HALO == 0
    grid = (batch, seq // tm)
    kv_width = N_KV_HEADS * HEAD_DIM

    def gains(i, extra_a, extra_b):
        return jnp.stack([pre_mix_g[i], post_mix_g[i], pre_ffn_g[i], post_ffn_g[i], ple_g[i],
                          ple_post_g[i], extra_a, extra_b]).astype(F32)

    zeros = jnp.zeros((D_MODEL,), F32)
    gains0 = gains(0, pool_scale[0], kv_g)
    gains1 = gains(1, zeros, zeros)
    w_kv_dup = jnp.concatenate([_dup_heads(w_kv[:, :kv_width]), _dup_heads(w_kv[:, kv_width:])], axis=1)
    bias, sink = _attention_tables(sinks[0])

    row_tile = lambda width: pl.BlockSpec((None, tm, width), lambda b, i: (b, i, 0))
    p_tile = lambda layer: pl.BlockSpec((None, None, tm, PLE_DIM), lambda b, i: (layer, b, i, 0))
    params = pltpu.CompilerParams(dimension_semantics=("arbitrary", "arbitrary"),
                                  vmem_limit_bytes=VMEM_LIMIT_BYTES)

    pool_halo = pl.BlockSpec((None, POOL_HALO, D_MODEL),
                             lambda b, i: (b, jnp.maximum(i * (tm // POOL_HALO) - 1, 0), 0))
    x3, k_dup, v_dup = pl.pallas_call(
        _layer0_body,
        grid=grid,
        in_specs=[row_tile(D_MODEL), pool_halo, p_tile(0), _resident((8, D_MODEL)),
                  _resident((len(POOL_WINDOWS), POOL_GROUP, POOL_GROUP)),
                  _resident((D_MODEL, 2 * D_FF)), _resident((D_FF, D_MODEL)),
                  _resident((D_MODEL, D_MODEL)), _resident((PLE_DIM, D_MODEL)),
                  _resident((D_MODEL, 2 * KV_DUP))],
        out_specs=[row_tile(D_MODEL), row_tile(KV_DUP), row_tile(KV_DUP)],
        out_shape=[jax.ShapeDtypeStruct((batch, seq, D_MODEL), F32),
                   jax.ShapeDtypeStruct((batch, seq, KV_DUP), BF16),
                   jax.ShapeDtypeStruct((batch, seq, KV_DUP), BF16)],
        scratch_shapes=[pltpu.VMEM((tm + POOL_HALO, D_MODEL), F32), pltpu.VMEM((tm, D_FF), BF16)],
        compiler_params=params,
        name="yoco_layer0",
    )(x, x, p, gains0, pool_w[0].astype(BF16), w_gu[0].astype(BF16), w_down[0].astype(BF16),
      w_ple_gate[0].astype(BF16), w_ple_proj[0].astype(BF16), w_kv_dup.astype(BF16))

    kv_halo = pl.BlockSpec((None, ATTN_BLOCK, KV_DUP),
                           lambda b, i: (b, jnp.maximum(i * (tm // ATTN_BLOCK) - 1, 0), 0))
    return pl.pallas_call(
        _layer1_body,
        grid=grid,
        in_specs=[row_tile(D_MODEL), row_tile(KV_DUP), kv_halo, row_tile(KV_DUP), kv_halo, p_tile(1),
                  _resident((8, D_MODEL)),
                  _resident((N_KV_HEADS, GQA_GROUP * ATTN_BLOCK, 2 * ATTN_BLOCK)),
                  _resident((N_KV_HEADS, GQA_GROUP * ATTN_BLOCK, LANES)),
                  _resident((D_MODEL, D_MODEL)), _resident((D_MODEL, D_MODEL)),
                  _resident((D_MODEL, 2 * D_FF)), _resident((D_FF, D_MODEL)),
                  _resident((D_MODEL, D_MODEL)), _resident((PLE_DIM, D_MODEL))],
        out_specs=row_tile(D_MODEL),
        out_shape=jax.ShapeDtypeStruct((batch, seq, D_MODEL), F32),
        scratch_shapes=[pltpu.VMEM((tm, D_MODEL), BF16), pltpu.VMEM((tm, D_FF), BF16)],
        compiler_params=params,
        name="yoco_layer1",
    )(x3, k_dup, k_dup, v_dup, v_dup, p, gains1, bias, sink, w_q[0].astype(BF16), w_o[0].astype(BF16),
      w_gu[1].astype(BF16), w_down[1].astype(BF16), w_ple_gate[1].astype(BF16),
      w_ple_proj[1].astype(BF16))
```

```python
import functools
from typing import Callable, NamedTuple

import jax
import jax.numpy as jnp
from jax import lax
from jax.experimental import pallas as pl
from jax.experimental.pallas import tpu as pltpu

D_MODEL = 1024
POOL_WINDOWS = (2, 4, 8, 16)
POOL_GROUP = D_MODEL // len(POOL_WINDOWS)
POOL_HALO = 16
HEAD_DIM = 64
N_HEADS = D_MODEL // HEAD_DIM
N_KV_HEADS = 4
GQA_GROUP = N_HEADS // N_KV_HEADS
ATTN_BLOCK = 128
D_FF = 2816
PLE_DIM = 256
EPS = 1e-6
NEG_INF = -1e30
LANES = 128
KV_WIDTH = N_KV_HEADS * HEAD_DIM
KV_TILES = KV_WIDTH // LANES
GROUP_ROWS = GQA_GROUP * ATTN_BLOCK

TILE_ROWS = 256
ROW_CHUNK = 32
COL_PIECE = 256
PIPE_STAGES = 3
VMEM_LIMIT_BYTES = 56 * 1024 * 1024

BF16 = jnp.bfloat16
F32 = jnp.float32

G_PRE_MIX, G_POST_MIX, G_PRE_FFN, G_POST_FFN, G_PLE, G_PLE_POST, G_POOL_SCALE, G_KV = range(8)


class _Step(NamedTuple):
    at: float
    run: Callable[[], None]
    name: str = ""


FFN_UP_DONE = "ffn_up_done"
DRAINED = "drained"
REFILL = "refill"


def _spread(span, runs, name="", last_name=""):
    lo, hi = span
    n = len(runs)
    return [_Step(lo + (i + 0.5) * (hi - lo) / n, run, last_name if (last_name and i == n - 1) else name)
            for i, run in enumerate(runs)]


def _program(*stages):
    for stage in stages:
        assert all(a.at <= b.at for a, b in zip(stage, stage[1:])), "a stage must be placed in order"
    order = sorted((s for stage in stages for s in stage), key=lambda s: s.at)
    names = [s.name for s in order]
    first_refill = names.index(REFILL)
    assert names.index(FFN_UP_DONE) < first_refill and names.index(DRAINED) < first_refill
    return order


def _row_chunks(rows):
    return [(lo, lo + ROW_CHUNK) for lo in range(0, rows, ROW_CHUNK)]


def _col_pieces(cols):
    return [(lo, lo + COL_PIECE) for lo in range(0, cols, COL_PIECE)]


def _rms(x, g):
    return x * lax.rsqrt(jnp.mean(x * x, axis=-1, keepdims=True) + EPS) * g


def _sigmoid(x):
    return 1.0 / (1.0 + jnp.exp(-x))


def _mm(a, b):
    return jnp.dot(a, b, preferred_element_type=F32)


def _zero_handoffs(step, *refs):
    @pl.when(step == 0)
    def _():
        for r in refs:
            r[...] = jnp.zeros(r.shape, r.dtype)


def _embed_steps(x1_ref, f_ref, p_ref, gain, wgate_ref, wproj_ref, x2_ref, hg_ref, e_ref, emit_rows,
                 at_residual, at_gated, at_finish):
    tm = f_ref.shape[0]

    def residual(lo, hi):
        x2 = x1_ref[lo:hi, :] + _rms(f_ref[lo:hi, :], gain(G_POST_FFN))
        x2_ref[lo:hi, :] = x2
        hg_ref[lo:hi, :] = _rms(x2, gain(G_PLE)).astype(BF16)

    def gated(lo, hi):
        gate = _sigmoid(_mm(hg_ref[...], wgate_ref[:, lo:hi]))
        e_ref[:, lo:hi] = _mm(p_ref[...].astype(BF16), wproj_ref[:, lo:hi]) * gate

    def finish(lo, hi):
        emit_rows(lo, hi, x2_ref[lo:hi, :] + _rms(e_ref[lo:hi, :], gain(G_PLE_POST)))

    return (_spread(at_residual, [functools.partial(residual, lo, hi) for lo, hi in _row_chunks(tm)],
                    last_name=DRAINED)
            + _spread(at_gated, [functools.partial(gated, lo, hi) for lo, hi in _col_pieces(D_MODEL)])
            + _spread(at_finish, [functools.partial(finish, lo, hi) for lo, hi in _row_chunks(tm)]))


def _ffn_steps(h_ref, f_ref, wgu_ref, wdown_ref, act_ref):
    def up(lo, hi):
        h = h_ref[...]
        gate = _mm(h, wgu_ref[:, lo:hi])
        act_ref[:, lo:hi] = (gate * _sigmoid(gate) * _mm(h, wgu_ref[:, D_FF + lo:D_FF + hi])).astype(BF16)

    def down(lo, hi):
        f_ref[:, lo:hi] = _mm(act_ref[...], wdown_ref[:, lo:hi])

    up_share = 2 * D_MODEL / (3 * D_MODEL)
    return (_spread((0.0, up_share), [functools.partial(up, lo, hi) for lo, hi in _col_pieces(D_FF)],
                    last_name=FFN_UP_DONE)
            + _spread((up_share, 1.0), [functools.partial(down, lo, hi) for lo, hi in _col_pieces(D_MODEL)],
                      name=REFILL))


def _handoff_steps(x_ref, y_ref, gain, x1_ref, hffn_ref, at):
    def handoff(lo, hi):
        x1 = x_ref[lo:hi, :] + _rms(y_ref[lo:hi, :], gain(G_POST_MIX))
        x1_ref[lo:hi, :] = x1
        hffn_ref[lo:hi, :] = _rms(x1, gain(G_PRE_FFN)).astype(BF16)

    return _spread(at, [functools.partial(handoff, lo, hi) for lo, hi in _row_chunks(x_ref.shape[0])],
                   name=REFILL)


def _layer0_body(x_ref, xhalo_ref, p_ref, gains_ref, poolw_ref, wgu_ref, wdown_ref, wgate_ref,
                 wproj_ref, wkv_ref, xo_ref, k_ref, v_ref,
                 x1ring_ref, hffn_ref, f_ref, act_ref, x2_ref, hg_ref, e_ref, hkv_ref,
                 hext_ref, d_ref, y_ref, *, tiles_per_seq, n_tiles):
    step = pl.program_id(0)
    tm = x_ref.shape[0]
    gain = lambda r: gains_ref[r:r + 1, :]
    _zero_handoffs(step, x1ring_ref, hffn_ref, f_ref)
    x1_ref = x1ring_ref.at[step % 2]

    def emit_rows(lo, hi, x3):
        xo_ref[lo:hi, :] = x3
        hkv_ref[lo:hi, :] = _rms(x3, gain(G_KV)).astype(BF16)

    def kv_proj(out_ref, col0):
        out_ref[...] = _mm(hkv_ref[...], wkv_ref[:, col0:col0 + KV_WIDTH]).astype(BF16)

    embed = _embed_steps(x1_ref, f_ref, p_ref, gain, wgate_ref, wproj_ref, x2_ref, hg_ref, e_ref, emit_rows,
                         at_residual=(0.0, 0.16), at_gated=(0.17, 0.40), at_finish=(0.42, 0.66))
    embed += _spread((0.70, 0.85), [functools.partial(kv_proj, k_ref, 0),
                                    functools.partial(kv_proj, v_ref, KV_WIDTH)])

    ffn = _ffn_steps(hffn_ref, f_ref, wgu_ref, wdown_ref, act_ref)

    seq_tile = jnp.minimum(step, n_tiles - 1) % tiles_per_seq

    def prenorm_halo():
        h_halo = _rms(xhalo_ref[...], gain(G_PRE_MIX))
        hext_ref[0:POOL_HALO, :] = jnp.where(seq_tile == 0, 0.0, h_halo)

    def prenorm(lo, hi):
        hext_ref[POOL_HALO + lo:POOL_HALO + hi, :] = _rms(x_ref[lo:hi, :], gain(G_PRE_MIX))

    def pool(lo, hi):
        pos1 = lax.broadcasted_iota(jnp.int32, (hi - lo, POOL_GROUP), 0) + (seq_tile * tm + lo + 1)
        for gi, w in enumerate(POOL_WINDOWS):
            c0 = gi * POOL_GROUP
            terms = [hext_ref[POOL_HALO + lo - j:POOL_HALO + hi - j, c0:c0 + POOL_GROUP] for j in range(w)]
            h = terms[0]
            while len(terms) > 1:
                terms = [terms[a] + terms[a + 1] for a in range(0, len(terms), 2)]
            cnt = jnp.minimum(pos1, w).astype(F32)
            d_ref[lo:hi, c0:c0 + POOL_GROUP] = (terms[0] / cnt - h).astype(BF16)

    def group_mix(gi):
        c0 = gi * POOL_GROUP
        y_ref[:, c0:c0 + POOL_GROUP] = (_mm(d_ref[:, c0:c0 + POOL_GROUP], poolw_ref[gi])
                                        * gain(G_POOL_SCALE)[:, c0:c0 + POOL_GROUP])

    mix = (_spread((0.0, 0.16), [prenorm_halo] + [functools.partial(prenorm, lo, hi) for lo, hi in _row_chunks(tm)])
           + _spread((0.17, 0.50), [functools.partial(pool, lo, hi) for lo, hi in _row_chunks(tm)])
           + _spread((0.51, 0.56), [functools.partial(group_mix, gi) for gi in range(len(POOL_WINDOWS))])
           + _handoff_steps(x_ref, y_ref, gain, x1_ref, hffn_ref, at=(0.68, 0.98)))

    for s in _program(embed, ffn, mix):
        s.run()


def _layer1_body(x_ref, k_ref, khalo_ref, v_ref, vhalo_ref, p_ref, gains_ref, bias_ref, sink_ref,
                 wq_ref, wo_ref, wgu_ref, wdown_ref, wgate_ref, wproj_ref, xo_ref,
                 x1ring_ref, hffn_ref, f_ref, act_ref, x2_ref, hg_ref, e_ref,
                 hq_ref, qa_ref, qb_ref, attn_ref, y_ref, *, tiles_per_seq, n_tiles):
    step = pl.program_id(0)
    tm = x_ref.shape[0]
    gain = lambda r: gains_ref[r:r + 1, :]
    _zero_handoffs(step, x1ring_ref, hffn_ref, f_ref)
    x1_ref = x1ring_ref.at[step % 2]

    def emit_rows(lo, hi, x3):
        xo_ref[lo:hi, :] = x3

    embed = _embed_steps(x1_ref, f_ref, p_ref, gain, wgate_ref, wproj_ref, x2_ref, hg_ref, e_ref, emit_rows,
                         at_residual=(0.0, 0.14), at_gated=(0.15, 0.35), at_finish=(0.36, 0.50))

    ffn = _ffn_steps(hffn_ref, f_ref, wgu_ref, wdown_ref, act_ref)

    seq_tile = jnp.minimum(step, n_tiles - 1) % tiles_per_seq

    def prenorm(lo, hi):
        hq_ref[lo:hi, :] = _rms(x_ref[lo:hi, :], gain(G_PRE_MIX)).astype(BF16)

    def q_proj(lo, hi):
        q = _mm(hq_ref[...], wq_ref[:, lo:hi]) * (HEAD_DIM ** -0.5)
        first = (lax.broadcasted_iota(jnp.int32, q.shape, 1) & (LANES - 1)) < HEAD_DIM
        qa_ref[:, lo:hi] = jnp.where(first, q, 0.0).astype(BF16)
        qb_ref[:, lo:hi] = jnp.where(first, 0.0, q).astype(BF16)

    def attend(j, kt, side):
        r0, c0 = j * ATTN_BLOCK, kt * LANES
        if j == 0:
            k_prev, v_prev = khalo_ref[:, c0:c0 + LANES], vhalo_ref[:, c0:c0 + LANES]
        else:
            k_prev = k_ref[r0 - ATTN_BLOCK:r0, c0:c0 + LANES]
            v_prev = v_ref[r0 - ATTN_BLOCK:r0, c0:c0 + LANES]
        k_win = jnp.concatenate([k_prev, k_ref[r0:r0 + ATTN_BLOCK, c0:c0 + LANES]], axis=0)
        v_win = jnp.concatenate([v_prev, v_ref[r0:r0 + ATTN_BLOCK, c0:c0 + LANES]], axis=0)
        q_ref = qa_ref if side == 0 else qb_ref
        tiles = [(kt * GQA_GROUP + g) * LANES for g in range(GQA_GROUP)]
        q4 = jnp.concatenate([q_ref[r0:r0 + ATTN_BLOCK, t:t + LANES] for t in tiles], axis=0)
        s = lax.dot_general(q4, k_win, (((1,), (1,)), ((), ())), preferred_element_type=F32)
        rows = slice((2 * kt + side) * GROUP_ROWS, (2 * kt + side + 1) * GROUP_ROWS)
        s = s + bias_ref[rows, :]
        s_prev, s_cur = s[:, :ATTN_BLOCK], s[:, ATTN_BLOCK:]
        if j == 0:
            s_prev = s_prev + jnp.where(seq_tile == 0, NEG_INF, 0.0)
        sink = sink_ref[rows, :]
        m = jnp.max(jnp.maximum(jnp.maximum(s_prev, s_cur), sink), axis=-1, keepdims=True)
        p_prev, p_cur = jnp.exp(s_prev - m), jnp.exp(s_cur - m)
        denom = jnp.sum(p_prev + p_cur, axis=-1, keepdims=True) + jnp.exp(sink - m)
        probs = jnp.concatenate([p_prev, p_cur], axis=-1).astype(BF16)
        o = (_mm(probs, v_win) / denom).astype(BF16)
        first = lax.broadcasted_iota(jnp.int32, (ATTN_BLOCK, LANES), 1) < HEAD_DIM
        for g, t in enumerate(tiles):
            og = o[g * ATTN_BLOCK:(g + 1) * ATTN_BLOCK]
            if side == 0:
                attn_ref[r0:r0 + ATTN_BLOCK, t:t + LANES] = og
            else:
                attn_ref[r0:r0 + ATTN_BLOCK, t:t + LANES] = jnp.where(
                    first, attn_ref[r0:r0 + ATTN_BLOCK, t:t + LANES], og)

    def o_proj(lo, hi):
        y_ref[:, lo:hi] = _mm(attn_ref[...], wo_ref[:, lo:hi])

    mix = (_spread((0.0, 0.10), [functools.partial(prenorm, lo, hi) for lo, hi in _row_chunks(tm)])
           + _spread((0.10, 0.25), [functools.partial(q_proj, lo, hi) for lo, hi in _col_pieces(D_MODEL)])
           + _spread((0.25, 0.63), [functools.partial(attend, j, kt, side) for j in range(tm // ATTN_BLOCK)
                                    for kt in range(KV_TILES) for side in range(2)])
           + _spread((0.64, 0.74), [functools.partial(o_proj, lo, hi) for lo, hi in _col_pieces(D_MODEL)])
           + _handoff_steps(x_ref, y_ref, gain, x1_ref, hffn_ref, at=(0.76, 0.99)))

    for s in _program(embed, ffn, mix):
        s.run()


def _resident(shape):
    return pl.BlockSpec(shape, lambda s: (0,) * len(shape), pipeline_mode=pl.Buffered(1))


def _head_order():
    order = []
    for kt in range(KV_TILES):
        for g in range(GQA_GROUP):
            order += [(2 * kt) * GQA_GROUP + g, (2 * kt + 1) * GQA_GROUP + g]
    return order


def _head_columns():
    return jnp.asarray([h * HEAD_DIM + c for h in _head_order() for c in range(HEAD_DIM)], jnp.int32)


def _attention_tables(sinks):
    qi = jnp.arange(ATTN_BLOCK)[:, None]
    si = jnp.arange(2 * ATTN_BLOCK)[None, :]
    rel = ATTN_BLOCK + qi - si
    valid = (rel >= 0) & (rel < ATTN_BLOCK)
    heads = jnp.arange(1, N_HEADS + 1, dtype=F32)
    slopes = jnp.exp2(-8.0 * heads / N_HEADS).reshape(N_KV_HEADS, GQA_GROUP)
    bias = jnp.where(valid[None, None], -slopes[:, :, None, None] * rel.astype(F32)[None, None], NEG_INF)
    sink = sinks.astype(F32).reshape(N_KV_HEADS, GQA_GROUP, 1, 1)
    sink = jnp.broadcast_to(sink, (N_KV_HEADS, GQA_GROUP, ATTN_BLOCK, LANES))
    return (bias.reshape(N_KV_HEADS * GROUP_ROWS, 2 * ATTN_BLOCK), sink.reshape(N_KV_HEADS * GROUP_ROWS, LANES))


def kernel(x, p, pre_mix_g, post_mix_g, pre_ffn_g, post_ffn_g, pool_w, pool_scale, kv_g, w_kv, w_q, sinks,
           w_o, w_gu, w_down, ple_g, w_ple_gate, w_ple_proj, ple_post_g):
    batch, seq, _ = x.shape
    tm = TILE_ROWS
    assert seq % tm == 0 and tm % ATTN_BLOCK == 0 and tm % POOL_HALO == 0 and tm % ROW_CHUNK == 0
    tokens = batch * seq
    n_tiles = tokens // tm
    grid = (n_tiles + PIPE_STAGES - 1,)
    static = dict(tiles_per_seq=seq // tm, n_tiles=n_tiles)

    def gains(i, extra_a, extra_b):
        return jnp.stack([pre_mix_g[i], post_mix_g[i], pre_ffn_g[i], post_ffn_g[i], ple_g[i],
                          ple_post_g[i], extra_a, extra_b]).astype(F32)

    zeros = jnp.zeros((D_MODEL,), F32)
    gains0 = gains(0, pool_scale[0], kv_g)
    gains1 = gains(1, zeros, zeros)
    head_cols = _head_columns()
    bias, sink = _attention_tables(sinks[0])
    x_rows = x.reshape(tokens, D_MODEL)
    p_rows = p.reshape(p.shape[0], tokens, PLE_DIM)

    mix_tile = lambda s: jnp.minimum(s, n_tiles - 1)
    out_tile = lambda s: jnp.clip(s - (PIPE_STAGES - 1), 0, n_tiles - 1)
    mix_rows = lambda width: pl.BlockSpec((tm, width), lambda s: (mix_tile(s), 0))
    out_rows = lambda width: pl.BlockSpec((tm, width), lambda s: (out_tile(s), 0))
    halo_rows = lambda rows, width: pl.BlockSpec(
        (rows, width), lambda s: (jnp.maximum(mix_tile(s) * (tm // rows) - 1, 0), 0))
    p_rows_spec = lambda layer: pl.BlockSpec((None, tm, PLE_DIM), lambda s: (layer, out_tile(s), 0))
    wide_f32, wide_bf16 = pltpu.VMEM((tm, D_MODEL), F32), pltpu.VMEM((tm, D_MODEL), BF16)
    common_scratch = [pltpu.VMEM((2, tm, D_MODEL), F32), wide_bf16, wide_f32, pltpu.VMEM((tm, D_FF), BF16),
                      wide_f32, wide_bf16, wide_f32]
    params = pltpu.CompilerParams(dimension_semantics=("arbitrary",), vmem_limit_bytes=VMEM_LIMIT_BYTES)

    x3, k, v = pl.pallas_call(
        functools.partial(_layer0_body, **static),
        grid=grid,
        in_specs=[mix_rows(D_MODEL), halo_rows(POOL_HALO, D_MODEL), p_rows_spec(0), _resident((8, D_MODEL)),
                  _resident((len(POOL_WINDOWS), POOL_GROUP, POOL_GROUP)),
                  _resident((D_MODEL, 2 * D_FF)), _resident((D_FF, D_MODEL)),
                  _resident((D_MODEL, D_MODEL)), _resident((PLE_DIM, D_MODEL)),
                  _resident((D_MODEL, 2 * KV_WIDTH))],
        out_specs=[out_rows(D_MODEL), out_rows(KV_WIDTH), out_rows(KV_WIDTH)],
        out_shape=[jax.ShapeDtypeStruct((tokens, D_MODEL), F32),
                   jax.ShapeDtypeStruct((tokens, KV_WIDTH), BF16),
                   jax.ShapeDtypeStruct((tokens, KV_WIDTH), BF16)],
        scratch_shapes=common_scratch + [wide_bf16, pltpu.VMEM((tm + POOL_HALO, D_MODEL), F32),
                                         wide_bf16, wide_f32],
        compiler_params=params,
        name="yoco_layer0",
    )(x_rows, x_rows, p_rows, gains0, pool_w[0].astype(BF16), w_gu[0].astype(BF16), w_down[0].astype(BF16),
      w_ple_gate[0].astype(BF16), w_ple_proj[0].astype(BF16), w_kv.astype(BF16))

    out = pl.pallas_call(
        functools.partial(_layer1_body, **static),
        grid=grid,
        in_specs=[mix_rows(D_MODEL), mix_rows(KV_WIDTH), halo_rows(ATTN_BLOCK, KV_WIDTH),
                  mix_rows(KV_WIDTH), halo_rows(ATTN_BLOCK, KV_WIDTH), p_rows_spec(1),
                  _resident((8, D_MODEL)), _resident((N_KV_HEADS * GROUP_ROWS, 2 * ATTN_BLOCK)),
                  _resident((N_KV_HEADS * GROUP_ROWS, LANES)),
                  _resident((D_MODEL, D_MODEL)), _resident((D_MODEL, D_MODEL)),
                  _resident((D_MODEL, 2 * D_FF)), _resident((D_FF, D_MODEL)),
                  _resident((D_MODEL, D_MODEL)), _resident((PLE_DIM, D_MODEL))],
        out_specs=out_rows(D_MODEL),
        out_shape=jax.ShapeDtypeStruct((tokens, D_MODEL), F32),
        scratch_shapes=common_scratch + [wide_bf16, wide_bf16, wide_bf16, wide_bf16, wide_f32],
        compiler_params=params,
        name="yoco_layer1",
    )(x3, k, k, v, v, p_rows, gains1, bias, sink, w_q[0][:, head_cols].astype(BF16),
      w_o[0][head_cols, :].astype(BF16), w_gu[1].astype(BF16), w_down[1].astype(BF16),
      w_ple_gate[1].astype(BF16), w_ple_proj[1].astype(BF16))
    return out.reshape(batch, seq, D_MODEL)
```

```python
import jax
import jax.numpy as jnp
from jax import lax
from jax.experimental import pallas as pl
from jax.experimental.pallas import tpu as pltpu

D_MODEL = 1024
POOL_WINDOWS = (2, 4, 8, 16)
POOL_GROUP = D_MODEL // len(POOL_WINDOWS)
POOL_HALO = 16
HEAD_DIM = 64
N_HEADS = D_MODEL // HEAD_DIM
N_KV_HEADS = 4
GQA_GROUP = N_HEADS // N_KV_HEADS
ATTN_BLOCK = 128
D_FF = 2816
PLE_DIM = 256
EPS = 1e-6
NEG_INF = -1e30
LANES = 128
KV_WIDTH = N_KV_HEADS * HEAD_DIM
KV_TILES = KV_WIDTH // LANES
HEADS_PER_KV_TILE = 2 * GQA_GROUP

TILE_ROWS = 512
FF_CHUNK = 256
VMEM_LIMIT_BYTES = 56 * 1024 * 1024

BF16 = jnp.bfloat16
F32 = jnp.float32

G_PRE_MIX, G_POST_MIX, G_PRE_FFN, G_POST_FFN, G_PLE, G_PLE_POST, G_POOL_SCALE, G_KV = range(8)


def _rms(x, g):
    return x * lax.rsqrt(jnp.mean(x * x, axis=-1, keepdims=True) + EPS) * g


def _sigmoid(x):
    return 1.0 / (1.0 + jnp.exp(-x))


def _mm(a, b):
    return jnp.dot(a, b, preferred_element_type=F32)


def _ffn_ple(x1, p_ref, gain, wgu_ref, wdown_ref, wgate_ref, wproj_ref, act_ref):
    h = _rms(x1, gain(G_PRE_FFN)).astype(BF16)
    for c in range(D_FF // FF_CHUNK):
        lo = c * FF_CHUNK
        gate = _mm(h, wgu_ref[:, lo:lo + FF_CHUNK])
        up = _mm(h, wgu_ref[:, D_FF + lo:D_FF + lo + FF_CHUNK])
        act_ref[:, lo:lo + FF_CHUNK] = (gate * _sigmoid(gate) * up).astype(BF16)
    x2 = x1 + _rms(_mm(act_ref[...], wdown_ref[...]), gain(G_POST_FFN))
    ple_gate = _sigmoid(_mm(_rms(x2, gain(G_PLE)).astype(BF16), wgate_ref[...]))
    e = _mm(p_ref[...].astype(BF16), wproj_ref[...]) * ple_gate
    return x2 + _rms(e, gain(G_PLE_POST))


def _layer0_body(x_ref, xhalo_ref, p_ref, gains_ref, poolw_ref, wgu_ref, wdown_ref, wgate_ref,
                 wproj_ref, wkv_ref, xo_ref, k_ref, v_ref, hext_ref, act_ref):
    tile = pl.program_id(1)
    tm = x_ref.shape[0]
    gains = gains_ref[...]
    gain = lambda r: gains[r:r + 1, :]

    x = x_ref[...]
    h = _rms(x, gain(G_PRE_MIX))
    h_halo = _rms(xhalo_ref[...], gain(G_PRE_MIX))
    hext_ref[0:POOL_HALO, :] = jnp.where(tile == 0, 0.0, h_halo)
    hext_ref[POOL_HALO:, :] = h

    pos1 = lax.broadcasted_iota(jnp.int32, (tm, POOL_GROUP), 0) + (tile * tm + 1)
    mixed = []
    for gi, w in enumerate(POOL_WINDOWS):
        lo = gi * POOL_GROUP
        terms = [hext_ref[POOL_HALO - j:POOL_HALO - j + tm, lo:lo + POOL_GROUP] for j in range(w)]
        while len(terms) > 1:
            terms = [terms[a] + terms[a + 1] for a in range(0, len(terms), 2)]
        cnt = jnp.minimum(pos1, w).astype(F32)
        d = terms[0] / cnt - h[:, lo:lo + POOL_GROUP]
        mixed.append(_mm(d.astype(BF16), poolw_ref[gi]))
    y = jnp.concatenate(mixed, axis=-1) * gain(G_POOL_SCALE)
    x1 = x + _rms(y, gain(G_POST_MIX))

    x3 = _ffn_ple(x1, p_ref, gain, wgu_ref, wdown_ref, wgate_ref, wproj_ref, act_ref)
    xo_ref[...] = x3

    kv = _mm(_rms(x3, gain(G_KV)).astype(BF16), wkv_ref[...])
    k_ref[...] = kv[:, :KV_WIDTH].astype(BF16)
    v_ref[...] = kv[:, KV_WIDTH:].astype(BF16)


def _layer1_body(x_ref, k_ref, khalo_ref, v_ref, vhalo_ref, p_ref, gains_ref, bias_ref, sink_ref,
                 wq_ref, wo_ref, wgu_ref, wdown_ref, wgate_ref, wproj_ref, xo_ref, attn_ref, act_ref):
    tile = pl.program_id(1)
    tm = x_ref.shape[0]
    gains = gains_ref[...]
    gain = lambda r: gains[r:r + 1, :]

    x = x_ref[...]
    q = _mm(_rms(x, gain(G_PRE_MIX)).astype(BF16), wq_ref[...]) * (HEAD_DIM ** -0.5)

    left = lax.broadcasted_iota(jnp.int32, (ATTN_BLOCK, LANES), 1) < HEAD_DIM
    no_prev = jnp.where(tile == 0, NEG_INF, 0.0)
    for j in range(tm // ATTN_BLOCK):
        r0 = j * ATTN_BLOCK
        for kt in range(KV_TILES):
            c0 = kt * LANES
            if j == 0:
                k_prev, v_prev = khalo_ref[:, c0:c0 + LANES], vhalo_ref[:, c0:c0 + LANES]
            else:
                k_prev = k_ref[r0 - ATTN_BLOCK:r0, c0:c0 + LANES]
                v_prev = v_ref[r0 - ATTN_BLOCK:r0, c0:c0 + LANES]
            k_win = jnp.concatenate([k_prev, k_ref[r0:r0 + ATTN_BLOCK, c0:c0 + LANES]], axis=0)
            v_win = jnp.concatenate([v_prev, v_ref[r0:r0 + ATTN_BLOCK, c0:c0 + LANES]], axis=0)
            q_tiles = [q[r0:r0 + ATTN_BLOCK, (kt * GQA_GROUP + g) * LANES:(kt * GQA_GROUP + g + 1) * LANES]
                       for g in range(GQA_GROUP)]
            q8 = jnp.concatenate([jnp.where(left, t, 0.0) for t in q_tiles]
                                 + [jnp.where(left, 0.0, t) for t in q_tiles], axis=0).astype(BF16)
            s = lax.dot_general(q8, k_win, (((1,), (1,)), ((), ())), preferred_element_type=F32)
            s = s + bias_ref[kt]
            s_prev, s_cur = s[:, :ATTN_BLOCK], s[:, ATTN_BLOCK:]
            if j == 0:
                s_prev = s_prev + no_prev
            sink = sink_ref[kt]
            m = jnp.max(jnp.maximum(jnp.maximum(s_prev, s_cur), sink), axis=-1, keepdims=True)
            p_prev, p_cur = jnp.exp(s_prev - m), jnp.exp(s_cur - m)
            denom = jnp.sum(p_prev + p_cur, axis=-1, keepdims=True) + jnp.exp(sink - m)
            probs = jnp.concatenate([p_prev, p_cur], axis=-1).astype(BF16)
            o = _mm(probs, v_win) / denom
            for g in range(GQA_GROUP):
                a = o[g * ATTN_BLOCK:(g + 1) * ATTN_BLOCK]
                b = o[(GQA_GROUP + g) * ATTN_BLOCK:(GQA_GROUP + g + 1) * ATTN_BLOCK]
                tile_lo = (kt * GQA_GROUP + g) * LANES
                attn_ref[r0:r0 + ATTN_BLOCK, tile_lo:tile_lo + LANES] = jnp.where(left, a, b).astype(BF16)

    x1 = x + _rms(_mm(attn_ref[...], wo_ref[...]), gain(G_POST_MIX))
    xo_ref[...] = _ffn_ple(x1, p_ref, gain, wgu_ref, wdown_ref, wgate_ref, wproj_ref, act_ref)


def _resident(shape):
    return pl.BlockSpec(shape, lambda b, i: (0,) * len(shape), pipeline_mode=pl.Buffered(1))


def _head_order():
    order = []
    for kt in range(KV_TILES):
        for g in range(GQA_GROUP):
            order += [(2 * kt) * GQA_GROUP + g, (2 * kt + 1) * GQA_GROUP + g]
    return order


def _head_columns():
    return jnp.asarray([h * HEAD_DIM + c for h in _head_order() for c in range(HEAD_DIM)], jnp.int32)


def _attention_tables(sinks):
    qi = jnp.arange(ATTN_BLOCK)[:, None]
    si = jnp.arange(2 * ATTN_BLOCK)[None, :]
    rel = ATTN_BLOCK + qi - si
    valid = (rel >= 0) & (rel < ATTN_BLOCK)
    heads = jnp.arange(1, N_HEADS + 1, dtype=F32)
    slopes = jnp.exp2(-8.0 * heads / N_HEADS).reshape(N_KV_HEADS, GQA_GROUP)
    bias = jnp.where(valid[None, None], -slopes[:, :, None, None] * rel.astype(F32)[None, None], NEG_INF)
    bias = bias.reshape(KV_TILES, HEADS_PER_KV_TILE * ATTN_BLOCK, 2 * ATTN_BLOCK)
    sink = sinks.astype(F32).reshape(N_KV_HEADS, GQA_GROUP, 1, 1)
    sink = jnp.broadcast_to(sink, (N_KV_HEADS, GQA_GROUP, ATTN_BLOCK, LANES))
    return bias, sink.reshape(KV_TILES, HEADS_PER_KV_TILE * ATTN_BLOCK, LANES)


def kernel(x, p, pre_mix_g, post_mix_g, pre_ffn_g, post_ffn_g, pool_w, pool_scale, kv_g, w_kv, w_q, sinks,
           w_o, w_gu, w_down, ple_g, w_ple_gate, w_ple_proj, ple_post_g):
    batch, seq, _ = x.shape
    tm = TILE_ROWS
    assert seq % tm == 0 and tm % ATTN_BLOCK == 0 and tm % POOL_HALO == 0
    grid = (batch, seq // tm)

    def gains(i, extra_a, extra_b):
        return jnp.stack([pre_mix_g[i], post_mix_g[i], pre_ffn_g[i], post_ffn_g[i], ple_g[i],
                          ple_post_g[i], extra_a, extra_b]).astype(F32)

    zeros = jnp.zeros((D_MODEL,), F32)
    gains0 = gains(0, pool_scale[0], kv_g)
    gains1 = gains(1, zeros, zeros)
    head_cols = _head_columns()
    bias, sink = _attention_tables(sinks[0])

    row_tile = lambda width: pl.BlockSpec((None, tm, width), lambda b, i: (b, i, 0))
    p_tile = lambda layer: pl.BlockSpec((None, None, tm, PLE_DIM), lambda b, i: (layer, b, i, 0))
    params = pltpu.CompilerParams(dimension_semantics=("arbitrary", "arbitrary"),
                                  vmem_limit_bytes=VMEM_LIMIT_BYTES)

    pool_halo = pl.BlockSpec((None, POOL_HALO, D_MODEL),
                             lambda b, i: (b, jnp.maximum(i * (tm // POOL_HALO) - 1, 0), 0))
    x3, k, v = pl.pallas_call(
        _layer0_body,
        grid=grid,
        in_specs=[row_tile(D_MODEL), pool_halo, p_tile(0), _resident((8, D_MODEL)),
                  _resident((len(POOL_WINDOWS), POOL_GROUP, POOL_GROUP)),
                  _resident((D_MODEL, 2 * D_FF)), _resident((D_FF, D_MODEL)),
                  _resident((D_MODEL, D_MODEL)), _resident((PLE_DIM, D_MODEL)),
                  _resident((D_MODEL, 2 * KV_WIDTH))],
        out_specs=[row_tile(D_MODEL), row_tile(KV_WIDTH), row_tile(KV_WIDTH)],
        out_shape=[jax.ShapeDtypeStruct((batch, seq, D_MODEL), F32),
                   jax.ShapeDtypeStruct((batch, seq, KV_WIDTH), BF16),
                   jax.ShapeDtypeStruct((batch, seq, KV_WIDTH), BF16)],
        scratch_shapes=[pltpu.VMEM((tm + POOL_HALO, D_MODEL), F32), pltpu.VMEM((tm, D_FF), BF16)],
        compiler_params=params,
        name="yoco_layer0",
    )(x, x, p, gains0, pool_w[0].astype(BF16), w_gu[0].astype(BF16), w_down[0].astype(BF16),
      w_ple_gate[0].astype(BF16), w_ple_proj[0].astype(BF16), w_kv.astype(BF16))

    kv_halo = pl.BlockSpec((None, ATTN_BLOCK, KV_WIDTH),
                           lambda b, i: (b, jnp.maximum(i * (tm // ATTN_BLOCK) - 1, 0), 0))
    table = lambda width: _resident((KV_TILES, HEADS_PER_KV_TILE * ATTN_BLOCK, width))
    return pl.pallas_call(
        _layer1_body,
        grid=grid,
        in_specs=[row_tile(D_MODEL), row_tile(KV_WIDTH), kv_halo, row_tile(KV_WIDTH), kv_halo, p_tile(1),
                  _resident((8, D_MODEL)), table(2 * ATTN_BLOCK), table(LANES),
                  _resident((D_MODEL, D_MODEL)), _resident((D_MODEL, D_MODEL)),
                  _resident((D_MODEL, 2 * D_FF)), _resident((D_FF, D_MODEL)),
                  _resident((D_MODEL, D_MODEL)), _resident((PLE_DIM, D_MODEL))],
        out_specs=row_tile(D_MODEL),
        out_shape=jax.ShapeDtypeStruct((batch, seq, D_MODEL), F32),
        scratch_shapes=[pltpu.VMEM((tm, D_MODEL), BF16), pltpu.VMEM((tm, D_FF), BF16)],
        compiler_params=params,
        name="yoco_layer1",
    )(x3, k, k, v, v, p, gains1, bias, sink, w_q[0][:, head_cols].astype(BF16),
      w_o[0][head_cols, :].astype(BF16), w_gu[1].astype(BF16), w_down[1].astype(BF16),
      w_ple_gate[1].astype(BF16), w_ple_proj[1].astype(BF16))
```

```python
import jax
import jax.numpy as jnp
from jax import lax
from jax.experimental import pallas as pl
from jax.experimental.pallas import tpu as pltpu

D_MODEL = 1024
POOL_WINDOWS = (2, 4, 8, 16)
POOL_GROUP = D_MODEL // len(POOL_WINDOWS)
POOL_HALO = 16
HEAD_DIM = 64
N_HEADS = D_MODEL // HEAD_DIM
N_KV_HEADS = 4
GQA_GROUP = N_HEADS // N_KV_HEADS
ATTN_BLOCK = 128
D_FF = 2816
PLE_DIM = 256
EPS = 1e-6
NEG_INF = -1e30
LANES = 128
KV_WIDTH = N_KV_HEADS * HEAD_DIM
KV_TILES = KV_WIDTH // LANES
HEADS_PER_KV_TILE = 2 * GQA_GROUP

CHAIN_ROWS = (256, 256)
TILE_ROWS = sum(CHAIN_ROWS)
FF_CHUNK = 256
COL_PIECE = 256
VMEM_LIMIT_BYTES = 56 * 1024 * 1024

BF16 = jnp.bfloat16
F32 = jnp.float32

G_PRE_MIX, G_POST_MIX, G_PRE_FFN, G_POST_FFN, G_PLE, G_PLE_POST, G_POOL_SCALE, G_KV = range(8)


def _rms(x, g):
    return x * lax.rsqrt(jnp.mean(x * x, axis=-1, keepdims=True) + EPS) * g


def _sigmoid(x):
    return 1.0 / (1.0 + jnp.exp(-x))


def _mm(a, b):
    return jnp.dot(a, b, preferred_element_type=F32)


def _run_staggered(chains, lead):
    chains = list(chains)
    count, done = [0] * len(chains), [False] * len(chains)
    while not all(done):
        for i, chain in enumerate(chains):
            if i > 0 and count[i - 1] < lead[i - 1] and not done[i - 1]:
                break
            if done[i]:
                continue
            try:
                next(chain)
                count[i] += 1
            except StopIteration:
                done[i] = True


def _col_pieces(width):
    return [(lo, lo + COL_PIECE) for lo in range(0, width, COL_PIECE)]


def _ffn_ple(x1, p, gain, wgu_ref, wdown_ref, wgate_ref, wproj_ref, act_ref):
    h = _rms(x1, gain(G_PRE_FFN)).astype(BF16)
    for lo in range(0, D_FF, FF_CHUNK):
        gate = _mm(h, wgu_ref[:, lo:lo + FF_CHUNK])
        up = _mm(h, wgu_ref[:, D_FF + lo:D_FF + lo + FF_CHUNK])
        act_ref[:, lo:lo + FF_CHUNK] = (gate * _sigmoid(gate) * up).astype(BF16)
        yield
    f = []
    for lo, hi in _col_pieces(D_MODEL):
        f.append(_mm(act_ref[...], wdown_ref[:, lo:hi]))
        yield
    x2 = x1 + _rms(jnp.concatenate(f, axis=-1), gain(G_POST_FFN))
    hg = _rms(x2, gain(G_PLE)).astype(BF16)
    pb = p.astype(BF16)
    yield
    e = []
    for lo, hi in _col_pieces(D_MODEL):
        e.append(_mm(pb, wproj_ref[:, lo:hi]) * _sigmoid(_mm(hg, wgate_ref[:, lo:hi])))
        yield
    return x2 + _rms(jnp.concatenate(e, axis=-1), gain(G_PLE_POST))


L0_MIX_PIECES = 2 + len(POOL_WINDOWS)


def _chain_rows(sub):
    r0 = sum(CHAIN_ROWS[:sub])
    return r0, CHAIN_ROWS[sub]


def _layer0_chain(sub, tile, gain, x_ref, xhalo_ref, p_ref, poolw_ref, wgu_ref, wdown_ref, wgate_ref,
                  wproj_ref, wkv_ref, xo_ref, k_ref, v_ref, hext_ref, act_ref):
    tm = x_ref.shape[0]
    r0, n = _chain_rows(sub)
    rows = slice(r0, r0 + n)
    hext = hext_ref.at[sub]
    x = x_ref[rows, :]
    h = _rms(x, gain(G_PRE_MIX))
    if r0 == 0:
        hext[0:POOL_HALO, :] = jnp.where(tile == 0, 0.0, _rms(xhalo_ref[...], gain(G_PRE_MIX)))
    else:
        hext[0:POOL_HALO, :] = _rms(x_ref[r0 - POOL_HALO:r0, :], gain(G_PRE_MIX))
    hext[POOL_HALO:POOL_HALO + n, :] = h
    yield

    pos1 = lax.broadcasted_iota(jnp.int32, (n, POOL_GROUP), 0) + (tile * tm + r0 + 1)
    mixed = []
    for gi, w in enumerate(POOL_WINDOWS):
        lo = gi * POOL_GROUP
        terms = [hext[POOL_HALO - j:POOL_HALO - j + n, lo:lo + POOL_GROUP] for j in range(w)]
        while len(terms) > 1:
            terms = [terms[a] + terms[a + 1] for a in range(0, len(terms), 2)]
        cnt = jnp.minimum(pos1, w).astype(F32)
        d = terms[0] / cnt - h[:, lo:lo + POOL_GROUP]
        mixed.append(_mm(d.astype(BF16), poolw_ref[gi]))
        yield
    y = jnp.concatenate(mixed, axis=-1) * gain(G_POOL_SCALE)
    x1 = x + _rms(y, gain(G_POST_MIX))
    yield

    x3 = yield from _ffn_ple(x1, p_ref[rows, :], gain, wgu_ref, wdown_ref, wgate_ref, wproj_ref,
                             act_ref.at[rows])
    xo_ref[rows, :] = x3
    hkv = _rms(x3, gain(G_KV)).astype(BF16)
    yield
    k_ref[rows, :] = _mm(hkv, wkv_ref[:, :KV_WIDTH]).astype(BF16)
    yield
    v_ref[rows, :] = _mm(hkv, wkv_ref[:, KV_WIDTH:]).astype(BF16)


def _layer0_body(x_ref, xhalo_ref, p_ref, gains_ref, *refs):
    tile = pl.program_id(1)
    gains = gains_ref[...]
    gain = lambda r: gains[r:r + 1, :]
    _run_staggered([_layer0_chain(sub, tile, gain, x_ref, xhalo_ref, p_ref, *refs)
                    for sub in range(len(CHAIN_ROWS))], lead=[L0_MIX_PIECES] * len(CHAIN_ROWS))


def _l1_mix_pieces(rows):
    return 2 + 2 * (D_MODEL // COL_PIECE) + (rows // ATTN_BLOCK) * KV_TILES


def _layer1_chain(sub, tile, gain, x_ref, k_ref, khalo_ref, v_ref, vhalo_ref, p_ref, bias_ref, sink_ref,
                  wq_ref, wo_ref, wgu_ref, wdown_ref, wgate_ref, wproj_ref, xo_ref, attn_ref, act_ref):
    s0, n = _chain_rows(sub)
    rows = slice(s0, s0 + n)
    x = x_ref[rows, :]
    hq = _rms(x, gain(G_PRE_MIX)).astype(BF16)
    yield
    q_pieces = []
    for lo, hi in _col_pieces(D_MODEL):
        q_pieces.append(_mm(hq, wq_ref[:, lo:hi]) * (HEAD_DIM ** -0.5))
        yield
    q = jnp.concatenate(q_pieces, axis=-1)

    left = lax.broadcasted_iota(jnp.int32, (ATTN_BLOCK, LANES), 1) < HEAD_DIM
    for j in range(n // ATTN_BLOCK):
        q0, r0 = j * ATTN_BLOCK, s0 + j * ATTN_BLOCK
        for kt in range(KV_TILES):
            c0 = kt * LANES
            if r0 == 0:
                k_prev, v_prev = khalo_ref[:, c0:c0 + LANES], vhalo_ref[:, c0:c0 + LANES]
            else:
                k_prev = k_ref[r0 - ATTN_BLOCK:r0, c0:c0 + LANES]
                v_prev = v_ref[r0 - ATTN_BLOCK:r0, c0:c0 + LANES]
            k_win = jnp.concatenate([k_prev, k_ref[r0:r0 + ATTN_BLOCK, c0:c0 + LANES]], axis=0)
            v_win = jnp.concatenate([v_prev, v_ref[r0:r0 + ATTN_BLOCK, c0:c0 + LANES]], axis=0)
            q_tiles = [q[q0:q0 + ATTN_BLOCK, (kt * GQA_GROUP + g) * LANES:(kt * GQA_GROUP + g + 1) * LANES]
                       for g in range(GQA_GROUP)]
            q8 = jnp.concatenate([jnp.where(left, t, 0.0) for t in q_tiles]
                                 + [jnp.where(left, 0.0, t) for t in q_tiles], axis=0).astype(BF16)
            s = lax.dot_general(q8, k_win, (((1,), (1,)), ((), ())), preferred_element_type=F32)
            s = s + bias_ref[kt]
            s_prev, s_cur = s[:, :ATTN_BLOCK], s[:, ATTN_BLOCK:]
            if r0 == 0:
                s_prev = s_prev + jnp.where(tile == 0, NEG_INF, 0.0)
            sink = sink_ref[kt]
            m = jnp.max(jnp.maximum(jnp.maximum(s_prev, s_cur), sink), axis=-1, keepdims=True)
            p_prev, p_cur = jnp.exp(s_prev - m), jnp.exp(s_cur - m)
            denom = jnp.sum(p_prev + p_cur, axis=-1, keepdims=True) + jnp.exp(sink - m)
            probs = jnp.concatenate([p_prev, p_cur], axis=-1).astype(BF16)
            o = _mm(probs, v_win) / denom
            for g in range(GQA_GROUP):
                a = o[g * ATTN_BLOCK:(g + 1) * ATTN_BLOCK]
                b = o[(GQA_GROUP + g) * ATTN_BLOCK:(GQA_GROUP + g + 1) * ATTN_BLOCK]
                tile_lo = (kt * GQA_GROUP + g) * LANES
                attn_ref[r0:r0 + ATTN_BLOCK, tile_lo:tile_lo + LANES] = jnp.where(left, a, b).astype(BF16)
            yield

    y = []
    for lo, hi in _col_pieces(D_MODEL):
        y.append(_mm(attn_ref[rows, :], wo_ref[:, lo:hi]))
        yield
    x1 = x + _rms(jnp.concatenate(y, axis=-1), gain(G_POST_MIX))
    yield
    xo_ref[rows, :] = yield from _ffn_ple(x1, p_ref[rows, :], gain, wgu_ref, wdown_ref, wgate_ref, wproj_ref,
                                          act_ref.at[rows])


def _layer1_body(x_ref, k_ref, khalo_ref, v_ref, vhalo_ref, p_ref, gains_ref, *refs):
    tile = pl.program_id(1)
    gains = gains_ref[...]
    gain = lambda r: gains[r:r + 1, :]
    _run_staggered([_layer1_chain(sub, tile, gain, x_ref, k_ref, khalo_ref, v_ref, vhalo_ref, p_ref, *refs)
                    for sub in range(len(CHAIN_ROWS))], lead=[_l1_mix_pieces(n) for n in CHAIN_ROWS])


def _resident(shape):
    return pl.BlockSpec(shape, lambda b, i: (0,) * len(shape), pipeline_mode=pl.Buffered(1))


def _head_order():
    order = []
    for kt in range(KV_TILES):
        for g in range(GQA_GROUP):
            order += [(2 * kt) * GQA_GROUP + g, (2 * kt + 1) * GQA_GROUP + g]
    return order


def _head_columns():
    return jnp.asarray([h * HEAD_DIM + c for h in _head_order() for c in range(HEAD_DIM)], jnp.int32)


def _attention_tables(sinks):
    qi = jnp.arange(ATTN_BLOCK)[:, None]
    si = jnp.arange(2 * ATTN_BLOCK)[None, :]
    rel = ATTN_BLOCK + qi - si
    valid = (rel >= 0) & (rel < ATTN_BLOCK)
    heads = jnp.arange(1, N_HEADS + 1, dtype=F32)
    slopes = jnp.exp2(-8.0 * heads / N_HEADS).reshape(N_KV_HEADS, GQA_GROUP)
    bias = jnp.where(valid[None, None], -slopes[:, :, None, None] * rel.astype(F32)[None, None], NEG_INF)
    bias = bias.reshape(KV_TILES, HEADS_PER_KV_TILE * ATTN_BLOCK, 2 * ATTN_BLOCK)
    sink = sinks.astype(F32).reshape(N_KV_HEADS, GQA_GROUP, 1, 1)
    sink = jnp.broadcast_to(sink, (N_KV_HEADS, GQA_GROUP, ATTN_BLOCK, LANES))
    return bias, sink.reshape(KV_TILES, HEADS_PER_KV_TILE * ATTN_BLOCK, LANES)


def kernel(x, p, pre_mix_g, post_mix_g, pre_ffn_g, post_ffn_g, pool_w, pool_scale, kv_g, w_kv, w_q, sinks,
           w_o, w_gu, w_down, ple_g, w_ple_gate, w_ple_proj, ple_post_g):
    batch, seq, _ = x.shape
    tm = TILE_ROWS
    assert seq % tm == 0 and all(n % ATTN_BLOCK == 0 and n % POOL_HALO == 0 for n in CHAIN_ROWS)
    grid = (batch, seq // tm)

    def gains(i, extra_a, extra_b):
        return jnp.stack([pre_mix_g[i], post_mix_g[i], pre_ffn_g[i], post_ffn_g[i], ple_g[i],
                          ple_post_g[i], extra_a, extra_b]).astype(F32)

    zeros = jnp.zeros((D_MODEL,), F32)
    gains0 = gains(0, pool_scale[0], kv_g)
    gains1 = gains(1, zeros, zeros)
    head_cols = _head_columns()
    bias, sink = _attention_tables(sinks[0])

    row_tile = lambda width: pl.BlockSpec((None, tm, width), lambda b, i: (b, i, 0))
    p_tile = lambda layer: pl.BlockSpec((None, None, tm, PLE_DIM), lambda b, i: (layer, b, i, 0))
    params = pltpu.CompilerParams(dimension_semantics=("arbitrary", "arbitrary"),
                                  vmem_limit_bytes=VMEM_LIMIT_BYTES)

    pool_halo = pl.BlockSpec((None, POOL_HALO, D_MODEL),
                             lambda b, i: (b, jnp.maximum(i * (tm // POOL_HALO) - 1, 0), 0))
    x3, k, v = pl.pallas_call(
        _layer0_body,
        grid=grid,
        in_specs=[row_tile(D_MODEL), pool_halo, p_tile(0), _resident((8, D_MODEL)),
                  _resident((len(POOL_WINDOWS), POOL_GROUP, POOL_GROUP)),
                  _resident((D_MODEL, 2 * D_FF)), _resident((D_FF, D_MODEL)),
                  _resident((D_MODEL, D_MODEL)), _resident((PLE_DIM, D_MODEL)),
                  _resident((D_MODEL, 2 * KV_WIDTH))],
        out_specs=[row_tile(D_MODEL), row_tile(KV_WIDTH), row_tile(KV_WIDTH)],
        out_shape=[jax.ShapeDtypeStruct((batch, seq, D_MODEL), F32),
                   jax.ShapeDtypeStruct((batch, seq, KV_WIDTH), BF16),
                   jax.ShapeDtypeStruct((batch, seq, KV_WIDTH), BF16)],
        scratch_shapes=[pltpu.VMEM((len(CHAIN_ROWS), max(CHAIN_ROWS) + POOL_HALO, D_MODEL), F32),
                        pltpu.VMEM((tm, D_FF), BF16)],
        compiler_params=params,
        name="yoco_layer0",
    )(x, x, p, gains0, pool_w[0].astype(BF16), w_gu[0].astype(BF16), w_down[0].astype(BF16),
      w_ple_gate[0].astype(BF16), w_ple_proj[0].astype(BF16), w_kv.astype(BF16))

    kv_halo = pl.BlockSpec((None, ATTN_BLOCK, KV_WIDTH),
                           lambda b, i: (b, jnp.maximum(i * (tm // ATTN_BLOCK) - 1, 0), 0))
    table = lambda width: _resident((KV_TILES, HEADS_PER_KV_TILE * ATTN_BLOCK, width))
    return pl.pallas_call(
        _layer1_body,
        grid=grid,
        in_specs=[row_tile(D_MODEL), row_tile(KV_WIDTH), kv_halo, row_tile(KV_WIDTH), kv_halo, p_tile(1),
                  _resident((8, D_MODEL)), table(2 * ATTN_BLOCK), table(LANES),
                  _resident((D_MODEL, D_MODEL)), _resident((D_MODEL, D_MODEL)),
                  _resident((D_MODEL, 2 * D_FF)), _resident((D_FF, D_MODEL)),
                  _resident((D_MODEL, D_MODEL)), _resident((PLE_DIM, D_MODEL))],
        out_specs=row_tile(D_MODEL),
        out_shape=jax.ShapeDtypeStruct((batch, seq, D_MODEL), F32),
        scratch_shapes=[pltpu.VMEM((tm, D_MODEL), BF16), pltpu.VMEM((tm, D_FF), BF16)],
        compiler_params=params,
        name="yoco_layer1",
    )(x3, k, k, v, v, p, gains1, bias, sink, w_q[0][:, head_cols].astype(BF16),
      w_o[0][head_cols, :].astype(BF16), w_gu[1].astype(BF16), w_down[1].astype(BF16),
      w_ple_gate[1].astype(BF16), w_ple_proj[1].astype(BF16))
```

```python
import jax
import jax.numpy as jnp
from jax import lax
from jax.experimental import pallas as pl
from jax.experimental.pallas import tpu as pltpu

D_MODEL = 1024
POOL_WINDOWS = (2, 4, 8, 16)
POOL_GROUP = D_MODEL // len(POOL_WINDOWS)
POOL_HALO = 16
HEAD_DIM = 64
N_HEADS = D_MODEL // HEAD_DIM
N_KV_HEADS = 4
GQA_GROUP = N_HEADS // N_KV_HEADS
ATTN_BLOCK = 128
D_FF = 2816
PLE_DIM = 256
EPS = 1e-6
NEG_INF = -1e30
LANES = 128
KV_WIDTH = N_KV_HEADS * HEAD_DIM
KV_TILES = KV_WIDTH // LANES
HEADS_PER_KV_TILE = 2 * GQA_GROUP

TILE_ROWS = 512
L0_CHAIN_ROWS = (256, 256)
L1_CHAIN_ROWS = (512,)
FF_CHUNK = 256
COL_PIECE = 256
STAGE_ROWS, STAGE_COLS = 1024, 512
VMEM_LIMIT_BYTES = 56 * 1024 * 1024

BF16 = jnp.bfloat16
F32 = jnp.float32

G_PRE_MIX, G_POST_MIX, G_PRE_FFN, G_POST_FFN, G_PLE, G_PLE_POST, G_POOL_SCALE, G_KV = range(8)


def _rms(x, g):
    return x * lax.rsqrt(jnp.mean(x * x, axis=-1, keepdims=True) + EPS) * g


def _sigmoid(x):
    return 1.0 / (1.0 + jnp.exp(-x))


def _mm(a, b):
    return jnp.dot(a, b, preferred_element_type=F32)


def _weight_chunks(src, dst):
    rows, cols = dst.shape
    chunks = []
    for r0 in range(0, rows, STAGE_ROWS):
        nr = min(STAGE_ROWS, rows - r0)
        for c0 in range(0, cols, STAGE_COLS):
            nc = min(STAGE_COLS, cols - c0)
            window = (pl.ds(r0, nr), pl.ds(c0, nc))
            chunks.append((src.at[window], dst.at[window], nr, nc))
    return chunks


def _convert_weights(pairs, stage_ref, sem):
    @pl.when((pl.program_id(0) == 0) & (pl.program_id(1) == 0))
    def _():
        chunks = [c for src, dst in pairs for c in _weight_chunks(src, dst)]

        def copy(i):
            src, _, nr, nc = chunks[i]
            return pltpu.make_async_copy(src, stage_ref.at[i % 2, pl.ds(0, nr), pl.ds(0, nc)], sem.at[i % 2])

        copy(0).start()
        for i, (_, dst, nr, nc) in enumerate(chunks):
            if i + 1 < len(chunks):
                copy(i + 1).start()
            copy(i).wait()
            dst[...] = stage_ref[i % 2, 0:nr, 0:nc].astype(BF16)


def _run_staggered(chains, lead):
    chains = list(chains)
    count, done = [0] * len(chains), [False] * len(chains)
    while not all(done):
        for i, chain in enumerate(chains):
            if i > 0 and count[i - 1] < lead[i - 1] and not done[i - 1]:
                break
            if done[i]:
                continue
            try:
                next(chain)
                count[i] += 1
            except StopIteration:
                done[i] = True


def _col_pieces(width):
    return [(lo, lo + COL_PIECE) for lo in range(0, width, COL_PIECE)]


def _chain_rows(chain_rows, sub):
    return sum(chain_rows[:sub]), chain_rows[sub]


def _ffn_ple(x1, p, gain, wgu_ref, wdown_ref, wgate_ref, wproj_ref, act_ref):
    h = _rms(x1, gain(G_PRE_FFN)).astype(BF16)
    for lo in range(0, D_FF, FF_CHUNK):
        gate = _mm(h, wgu_ref[:, lo:lo + FF_CHUNK])
        up = _mm(h, wgu_ref[:, D_FF + lo:D_FF + lo + FF_CHUNK])
        act_ref[:, lo:lo + FF_CHUNK] = (gate * _sigmoid(gate) * up).astype(BF16)
        yield
    f = []
    for lo, hi in _col_pieces(D_MODEL):
        f.append(_mm(act_ref[...], wdown_ref[:, lo:hi]))
        yield
    x2 = x1 + _rms(jnp.concatenate(f, axis=-1), gain(G_POST_FFN))
    hg = _rms(x2, gain(G_PLE)).astype(BF16)
    pb = p.astype(BF16)
    yield
    e = []
    for lo, hi in _col_pieces(D_MODEL):
        e.append(_mm(pb, wproj_ref[:, lo:hi]) * _sigmoid(_mm(hg, wgate_ref[:, lo:hi])))
        yield
    return x2 + _rms(jnp.concatenate(e, axis=-1), gain(G_PLE_POST))


L0_MIX_PIECES = 2 + len(POOL_WINDOWS)


def _layer0_chain(sub, tile, gain, x_ref, xhalo_ref, p_ref, xo_ref, k_ref, v_ref, hext_ref, act_ref,
                  poolw_ref, wgu_ref, wdown_ref, wgate_ref, wproj_ref, wkv_ref):
    tm = x_ref.shape[0]
    r0, n = _chain_rows(L0_CHAIN_ROWS, sub)
    rows = slice(r0, r0 + n)
    hext = hext_ref.at[sub]
    x = x_ref[rows, :]
    h = _rms(x, gain(G_PRE_MIX))
    if r0 == 0:
        hext[0:POOL_HALO, :] = jnp.where(tile == 0, 0.0, _rms(xhalo_ref[...], gain(G_PRE_MIX)))
    else:
        hext[0:POOL_HALO, :] = _rms(x_ref[r0 - POOL_HALO:r0, :], gain(G_PRE_MIX))
    hext[POOL_HALO:POOL_HALO + n, :] = h
    yield

    pos1 = lax.broadcasted_iota(jnp.int32, (n, POOL_GROUP), 0) + (tile * tm + r0 + 1)
    mixed = []
    for gi, w in enumerate(POOL_WINDOWS):
        lo = gi * POOL_GROUP
        terms = [hext[POOL_HALO - j:POOL_HALO - j + n, lo:lo + POOL_GROUP] for j in range(w)]
        while len(terms) > 1:
            terms = [terms[a] + terms[a + 1] for a in range(0, len(terms), 2)]
        cnt = jnp.minimum(pos1, w).astype(F32)
        d = terms[0] / cnt - h[:, lo:lo + POOL_GROUP]
        mixed.append(_mm(d.astype(BF16), poolw_ref[lo:lo + POOL_GROUP, :]))
        yield
    y = jnp.concatenate(mixed, axis=-1) * gain(G_POOL_SCALE)
    x1 = x + _rms(y, gain(G_POST_MIX))
    yield

    x3 = yield from _ffn_ple(x1, p_ref[rows, :], gain, wgu_ref, wdown_ref, wgate_ref, wproj_ref,
                             act_ref.at[rows])
    xo_ref[rows, :] = x3
    hkv = _rms(x3, gain(G_KV)).astype(BF16)
    yield
    k_ref[rows, :] = _mm(hkv, wkv_ref[:, :KV_WIDTH]).astype(BF16)
    yield
    v_ref[rows, :] = _mm(hkv, wkv_ref[:, KV_WIDTH:]).astype(BF16)


def _layer0_body(x_ref, xhalo_ref, p_ref, gains_ref, poolw_hbm, wgu_hbm, wdown_hbm, wgate_hbm, wproj_hbm,
                 wkv_hbm, xo_ref, k_ref, v_ref, hext_ref, act_ref, poolw_ref, wgu_ref, wdown_ref, wgate_ref,
                 wproj_ref, wkv_ref, stage_ref, sem):
    weights = (poolw_ref, wgu_ref, wdown_ref, wgate_ref, wproj_ref, wkv_ref)
    _convert_weights(zip((poolw_hbm.at[0], wgu_hbm.at[0], wdown_hbm.at[0], wgate_hbm.at[0], wproj_hbm.at[0],
                          wkv_hbm), weights), stage_ref, sem)
    tile = pl.program_id(1)
    gains = gains_ref[...]
    gain = lambda r: gains[r:r + 1, :]
    _run_staggered([_layer0_chain(sub, tile, gain, x_ref, xhalo_ref, p_ref, xo_ref, k_ref, v_ref, hext_ref,
                                  act_ref, *weights) for sub in range(len(L0_CHAIN_ROWS))],
                   lead=[L0_MIX_PIECES] * len(L0_CHAIN_ROWS))


def _l1_mix_pieces(rows):
    return 2 + 2 * (D_MODEL // COL_PIECE) + (rows // ATTN_BLOCK) * KV_TILES


def _layer1_chain(sub, tile, gain, x_ref, k_ref, khalo_ref, v_ref, vhalo_ref, p_ref, bias_ref, sink_ref,
                  xo_ref, attn_ref, act_ref, wq_ref, wo_ref, wgu_ref, wdown_ref, wgate_ref, wproj_ref):
    s0, n = _chain_rows(L1_CHAIN_ROWS, sub)
    rows = slice(s0, s0 + n)
    x = x_ref[rows, :]
    hq = _rms(x, gain(G_PRE_MIX)).astype(BF16)
    yield
    q_pieces = []
    for lo, hi in _col_pieces(D_MODEL):
        q_pieces.append(_mm(hq, wq_ref[:, lo:hi]) * (HEAD_DIM ** -0.5))
        yield
    q = jnp.concatenate(q_pieces, axis=-1)

    left = lax.broadcasted_iota(jnp.int32, (ATTN_BLOCK, LANES), 1) < HEAD_DIM
    for j in range(n // ATTN_BLOCK):
        q0, r0 = j * ATTN_BLOCK, s0 + j * ATTN_BLOCK
        for kt in range(KV_TILES):
            c0 = kt * LANES
            if r0 == 0:
                k_prev, v_prev = khalo_ref[:, c0:c0 + LANES], vhalo_ref[:, c0:c0 + LANES]
            else:
                k_prev = k_ref[r0 - ATTN_BLOCK:r0, c0:c0 + LANES]
                v_prev = v_ref[r0 - ATTN_BLOCK:r0, c0:c0 + LANES]
            k_win = jnp.concatenate([k_prev, k_ref[r0:r0 + ATTN_BLOCK, c0:c0 + LANES]], axis=0)
            v_win = jnp.concatenate([v_prev, v_ref[r0:r0 + ATTN_BLOCK, c0:c0 + LANES]], axis=0)
            q_tiles = [q[q0:q0 + ATTN_BLOCK, (kt * GQA_GROUP + g) * LANES:(kt * GQA_GROUP + g + 1) * LANES]
                       for g in range(GQA_GROUP)]
            q8 = jnp.concatenate([jnp.where(left, t, 0.0) for t in q_tiles]
                                 + [jnp.where(left, 0.0, t) for t in q_tiles], axis=0).astype(BF16)
            s = lax.dot_general(q8, k_win, (((1,), (1,)), ((), ())), preferred_element_type=F32)
            s = s + bias_ref[kt]
            s_prev, s_cur = s[:, :ATTN_BLOCK], s[:, ATTN_BLOCK:]
            if r0 == 0:
                s_prev = s_prev + jnp.where(tile == 0, NEG_INF, 0.0)
            sink = sink_ref[kt]
            m = jnp.max(jnp.maximum(jnp.maximum(s_prev, s_cur), sink), axis=-1, keepdims=True)
            p_prev, p_cur = jnp.exp(s_prev - m), jnp.exp(s_cur - m)
            denom = jnp.sum(p_prev + p_cur, axis=-1, keepdims=True) + jnp.exp(sink - m)
            probs = jnp.concatenate([p_prev, p_cur], axis=-1).astype(BF16)
            o = _mm(probs, v_win) / denom
            for g in range(GQA_GROUP):
                a = o[g * ATTN_BLOCK:(g + 1) * ATTN_BLOCK]
                b = o[(GQA_GROUP + g) * ATTN_BLOCK:(GQA_GROUP + g + 1) * ATTN_BLOCK]
                tile_lo = (kt * GQA_GROUP + g) * LANES
                attn_ref[r0:r0 + ATTN_BLOCK, tile_lo:tile_lo + LANES] = jnp.where(left, a, b).astype(BF16)
            yield

    y = []
    for lo, hi in _col_pieces(D_MODEL):
        y.append(_mm(attn_ref[rows, :], wo_ref[:, lo:hi]))
        yield
    x1 = x + _rms(jnp.concatenate(y, axis=-1), gain(G_POST_MIX))
    yield
    xo_ref[rows, :] = yield from _ffn_ple(x1, p_ref[rows, :], gain, wgu_ref, wdown_ref, wgate_ref, wproj_ref,
                                          act_ref.at[rows])


def _layer1_body(x_ref, k_ref, khalo_ref, v_ref, vhalo_ref, p_ref, gains_ref, bias_ref, sink_ref,
                 wq_hbm, wo_hbm, wgu_hbm, wdown_hbm, wgate_hbm, wproj_hbm, xo_ref, attn_ref, act_ref,
                 wq_ref, wo_ref, wgu_ref, wdown_ref, wgate_ref, wproj_ref, stage_ref, sem):
    weights = (wq_ref, wo_ref, wgu_ref, wdown_ref, wgate_ref, wproj_ref)
    _convert_weights(zip((wq_hbm, wo_hbm, wgu_hbm.at[1], wdown_hbm.at[1], wgate_hbm.at[1], wproj_hbm.at[1]),
                         weights), stage_ref, sem)
    tile = pl.program_id(1)
    gains = gains_ref[...]
    gain = lambda r: gains[r:r + 1, :]
    _run_staggered([_layer1_chain(sub, tile, gain, x_ref, k_ref, khalo_ref, v_ref, vhalo_ref, p_ref, bias_ref,
                                  sink_ref, xo_ref, attn_ref, act_ref, *weights)
                    for sub in range(len(L1_CHAIN_ROWS))], lead=[_l1_mix_pieces(n) for n in L1_CHAIN_ROWS])


def _resident(shape):
    return pl.BlockSpec(shape, lambda b, i: (0,) * len(shape), pipeline_mode=pl.Buffered(1))


def _head_order():
    order = []
    for kt in range(KV_TILES):
        for g in range(GQA_GROUP):
            order += [(2 * kt) * GQA_GROUP + g, (2 * kt + 1) * GQA_GROUP + g]
    return order


def _head_columns():
    return jnp.asarray([h * HEAD_DIM + c for h in _head_order() for c in range(HEAD_DIM)], jnp.int32)


def _attention_tables(sinks):
    qi = jnp.arange(ATTN_BLOCK)[:, None]
    si = jnp.arange(2 * ATTN_BLOCK)[None, :]
    rel = ATTN_BLOCK + qi - si
    valid = (rel >= 0) & (rel < ATTN_BLOCK)
    heads = jnp.arange(1, N_HEADS + 1, dtype=F32)
    slopes = jnp.exp2(-8.0 * heads / N_HEADS).reshape(N_KV_HEADS, GQA_GROUP)
    bias = jnp.where(valid[None, None], -slopes[:, :, None, None] * rel.astype(F32)[None, None], NEG_INF)
    bias = bias.reshape(KV_TILES, HEADS_PER_KV_TILE * ATTN_BLOCK, 2 * ATTN_BLOCK)
    sink = sinks.astype(F32).reshape(N_KV_HEADS, GQA_GROUP, 1, 1)
    sink = jnp.broadcast_to(sink, (N_KV_HEADS, GQA_GROUP, ATTN_BLOCK, LANES))
    return bias, sink.reshape(KV_TILES, HEADS_PER_KV_TILE * ATTN_BLOCK, LANES)


def kernel(x, p, pre_mix_g, post_mix_g, pre_ffn_g, post_ffn_g, pool_w, pool_scale, kv_g, w_kv, w_q, sinks,
           w_o, w_gu, w_down, ple_g, w_ple_gate, w_ple_proj, ple_post_g):
    batch, seq, _ = x.shape
    tm = TILE_ROWS
    assert seq % tm == 0 and sum(L0_CHAIN_ROWS) == tm and sum(L1_CHAIN_ROWS) == tm
    assert all(n % ATTN_BLOCK == 0 and n % POOL_HALO == 0 for n in L0_CHAIN_ROWS + L1_CHAIN_ROWS)
    grid = (batch, seq // tm)

    def gains(i, extra_a, extra_b):
        return jnp.stack([pre_mix_g[i], post_mix_g[i], pre_ffn_g[i], post_ffn_g[i], ple_g[i],
                          ple_post_g[i], extra_a, extra_b]).astype(F32)

    zeros = jnp.zeros((D_MODEL,), F32)
    gains0 = gains(0, pool_scale[0], kv_g)
    gains1 = gains(1, zeros, zeros)
    head_cols = _head_columns()
    bias, sink = _attention_tables(sinks[0])
    pool_w_rows = pool_w.reshape(pool_w.shape[0], D_MODEL, POOL_GROUP)

    row_tile = lambda width: pl.BlockSpec((None, tm, width), lambda b, i: (b, i, 0))
    p_tile = lambda layer: pl.BlockSpec((None, None, tm, PLE_DIM), lambda b, i: (layer, b, i, 0))
    in_hbm = pl.BlockSpec(memory_space=pl.ANY)
    bf16_weight = lambda rows, cols: pltpu.VMEM((rows, cols), BF16)
    ffn_weights = [bf16_weight(D_MODEL, 2 * D_FF), bf16_weight(D_FF, D_MODEL), bf16_weight(D_MODEL, D_MODEL),
                   bf16_weight(PLE_DIM, D_MODEL)]
    staging = [pltpu.VMEM((2, STAGE_ROWS, STAGE_COLS), F32), pltpu.SemaphoreType.DMA((2,))]
    params = pltpu.CompilerParams(dimension_semantics=("arbitrary", "arbitrary"),
                                  vmem_limit_bytes=VMEM_LIMIT_BYTES)

    pool_halo = pl.BlockSpec((None, POOL_HALO, D_MODEL),
                             lambda b, i: (b, jnp.maximum(i * (tm // POOL_HALO) - 1, 0), 0))
    x3, k, v = pl.pallas_call(
        _layer0_body,
        grid=grid,
        in_specs=[row_tile(D_MODEL), pool_halo, p_tile(0), _resident((8, D_MODEL))] + [in_hbm] * 6,
        out_specs=[row_tile(D_MODEL), row_tile(KV_WIDTH), row_tile(KV_WIDTH)],
        out_shape=[jax.ShapeDtypeStruct((batch, seq, D_MODEL), F32),
                   jax.ShapeDtypeStruct((batch, seq, KV_WIDTH), BF16),
                   jax.ShapeDtypeStruct((batch, seq, KV_WIDTH), BF16)],
        scratch_shapes=[pltpu.VMEM((len(L0_CHAIN_ROWS), max(L0_CHAIN_ROWS) + POOL_HALO, D_MODEL), F32),
                        pltpu.VMEM((tm, D_FF), BF16), bf16_weight(D_MODEL, POOL_GROUP)] + ffn_weights
                       + [bf16_weight(D_MODEL, 2 * KV_WIDTH)] + staging,
        compiler_params=params,
        name="yoco_layer0",
    )(x, x, p, gains0, pool_w_rows, w_gu, w_down, w_ple_gate, w_ple_proj, w_kv)

    kv_halo = pl.BlockSpec((None, ATTN_BLOCK, KV_WIDTH),
                           lambda b, i: (b, jnp.maximum(i * (tm // ATTN_BLOCK) - 1, 0), 0))
    table = lambda width: _resident((KV_TILES, HEADS_PER_KV_TILE * ATTN_BLOCK, width))
    return pl.pallas_call(
        _layer1_body,
        grid=grid,
        in_specs=[row_tile(D_MODEL), row_tile(KV_WIDTH), kv_halo, row_tile(KV_WIDTH), kv_halo, p_tile(1),
                  _resident((8, D_MODEL)), table(2 * ATTN_BLOCK), table(LANES)] + [in_hbm] * 6,
        out_specs=row_tile(D_MODEL),
        out_shape=jax.ShapeDtypeStruct((batch, seq, D_MODEL), F32),
        scratch_shapes=[pltpu.VMEM((tm, D_MODEL), BF16), pltpu.VMEM((tm, D_FF), BF16),
                        bf16_weight(D_MODEL, D_MODEL), bf16_weight(D_MODEL, D_MODEL)] + ffn_weights + staging,
        compiler_params=params,
        name="yoco_layer1",
    )(x3, k, k, v, v, p, gains1, bias, sink, w_q[0][:, head_cols], w_o[0][head_cols, :],
      w_gu, w_down, w_ple_gate, w_ple_proj)
```

```python
import functools

import jax
import jax.numpy as jnp
from jax import lax
from jax.experimental import pallas as pl
from jax.experimental.pallas import tpu as pltpu

D_MODEL = 1024
POOL_WINDOWS = (2, 4, 8, 16)
POOL_GROUP = D_MODEL // len(POOL_WINDOWS)
POOL_HALO = 16
HEAD_DIM = 64
N_HEADS = D_MODEL // HEAD_DIM
N_KV_HEADS = 4
GQA_GROUP = N_HEADS // N_KV_HEADS
ATTN_BLOCK = 128
D_FF = 2816
PLE_DIM = 256
EPS = 1e-6
NEG_INF = -1e30
LANES = 128
KV_WIDTH = N_KV_HEADS * HEAD_DIM
KV_TILES = KV_WIDTH // LANES
HEADS_PER_KV_TILE = 2 * GQA_GROUP

TILE_ROWS = 512
HALF_ROWS = TILE_ROWS // 2
FF_CHUNK = 256
COL_PIECE = 256
STAGE_ROWS, STAGE_COLS = 1024, 512
VMEM_LIMIT_BYTES = 56 * 1024 * 1024

BF16 = jnp.bfloat16
F32 = jnp.float32

G_PRE_MIX, G_POST_MIX, G_PRE_FFN, G_POST_FFN, G_PLE, G_PLE_POST, G_POOL_SCALE, G_KV = range(8)


def _rms(x, g):
    return x * lax.rsqrt(jnp.mean(x * x, axis=-1, keepdims=True) + EPS) * g


def _sigmoid(x):
    return 1.0 / (1.0 + jnp.exp(-x))


def _mm(a, b):
    return jnp.dot(a, b, preferred_element_type=F32)


def _weight_chunks(src, dst):
    rows, cols = dst.shape
    chunks = []
    for r0 in range(0, rows, STAGE_ROWS):
        nr = min(STAGE_ROWS, rows - r0)
        for c0 in range(0, cols, STAGE_COLS):
            nc = min(STAGE_COLS, cols - c0)
            window = (pl.ds(r0, nr), pl.ds(c0, nc))
            chunks.append((src.at[window], dst.at[window], nr, nc))
    return chunks


def _convert_weights(first_step, pairs, stage_ref, sem):
    @pl.when(first_step)
    def _():
        chunks = [c for src, dst in pairs for c in _weight_chunks(src, dst)]

        def copy(i):
            src, _, nr, nc = chunks[i]
            return pltpu.make_async_copy(src, stage_ref.at[i % 2, pl.ds(0, nr), pl.ds(0, nc)], sem.at[i % 2])

        copy(0).start()
        for i, (_, dst, nr, nc) in enumerate(chunks):
            if i + 1 < len(chunks):
                copy(i + 1).start()
            copy(i).wait()
            dst[...] = stage_ref[i % 2, 0:nr, 0:nc].astype(BF16)


def _run_staggered(chains, lead):
    chains = list(chains)
    count, done = [0] * len(chains), [False] * len(chains)
    while not all(done):
        for i, chain in enumerate(chains):
            if i > 0 and count[i - 1] < lead[i - 1] and not done[i - 1]:
                break
            if done[i]:
                continue
            try:
                next(chain)
                count[i] += 1
            except StopIteration:
                done[i] = True


def _col_pieces(width):
    return [(lo, lo + COL_PIECE) for lo in range(0, width, COL_PIECE)]


def _ffn_up(x1, gain, wgu_ref, act_ref):
    h = _rms(x1, gain(G_PRE_FFN)).astype(BF16)
    for lo in range(0, D_FF, FF_CHUNK):
        gate = _mm(h, wgu_ref[:, lo:lo + FF_CHUNK])
        up = _mm(h, wgu_ref[:, D_FF + lo:D_FF + lo + FF_CHUNK])
        act_ref[:, lo:lo + FF_CHUNK] = (gate * _sigmoid(gate) * up).astype(BF16)
        yield


def _ffn_down(act_ref, wdown_ref):
    f = []
    for lo, hi in _col_pieces(D_MODEL):
        f.append(_mm(act_ref[...], wdown_ref[:, lo:hi]))
        yield
    return jnp.concatenate(f, axis=-1)


def _embed(x1, f, p, gain, wgate_ref, wproj_ref):
    x2 = x1 + _rms(f, gain(G_POST_FFN))
    hg = _rms(x2, gain(G_PLE)).astype(BF16)
    pb = p.astype(BF16)
    yield
    e = []
    for lo, hi in _col_pieces(D_MODEL):
        e.append(_mm(pb, wproj_ref[:, lo:hi]) * _sigmoid(_mm(hg, wgate_ref[:, lo:hi])))
        yield
    return x2 + _rms(jnp.concatenate(e, axis=-1), gain(G_PLE_POST))


L0_MIX_PIECES = 2 + len(POOL_WINDOWS)
L0_START_B_PIECES = L0_MIX_PIECES + D_FF // FF_CHUNK
L0_CHAIN_PIECES = L0_START_B_PIECES + 2 * (D_MODEL // COL_PIECE) + 1 + 3


def _pool_mixer(x, h_hist, pos0, gain, hext, poolw_ref):
    n = x.shape[0]
    h = _rms(x, gain(G_PRE_MIX))
    hext[0:POOL_HALO, :] = h_hist
    hext[POOL_HALO:POOL_HALO + n, :] = h
    yield
    pos1 = lax.broadcasted_iota(jnp.int32, (n, POOL_GROUP), 0) + (pos0 + 1)
    mixed = []
    for gi, w in enumerate(POOL_WINDOWS):
        lo = gi * POOL_GROUP
        terms = [hext[POOL_HALO - j:POOL_HALO - j + n, lo:lo + POOL_GROUP] for j in range(w)]
        while len(terms) > 1:
            terms = [terms[a] + terms[a + 1] for a in range(0, len(terms), 2)]
        cnt = jnp.minimum(pos1, w).astype(F32)
        d = terms[0] / cnt - h[:, lo:lo + POOL_GROUP]
        mixed.append(_mm(d.astype(BF16), poolw_ref[lo:lo + POOL_GROUP, :]))
        yield
    y = jnp.concatenate(mixed, axis=-1) * gain(G_POOL_SCALE)
    x1 = x + _rms(y, gain(G_POST_MIX))
    yield
    return x1


def _emit_layer0(x3, gain, wkv_ref, xo_ref, k_ref, v_ref):
    xo_ref[...] = x3
    hkv = _rms(x3, gain(G_KV)).astype(BF16)
    yield
    k_ref[...] = _mm(hkv, wkv_ref[:, :KV_WIDTH]).astype(BF16)
    yield
    v_ref[...] = _mm(hkv, wkv_ref[:, KV_WIDTH:]).astype(BF16)


def _layer0_body(x_ref, xhalo_ref, pa_ref, pb_ref, gains_ref, poolw_hbm, wgu_hbm, wdown_hbm, wgate_hbm,
                 wproj_hbm, wkv_hbm, xoa_ref, ka_ref, va_ref, xob_ref, kb_ref, vb_ref,
                 hext_ref, acta_ref, actb_ref, x1b_ref, poolw_ref, wgu_ref, wdown_ref, wgate_ref, wproj_ref,
                 wkv_ref, stage_ref, sem, *, tiles_per_seq, n_tiles):
    step = pl.program_id(0)
    _convert_weights(step == 0, zip((poolw_hbm.at[0], wgu_hbm.at[0], wdown_hbm.at[0], wgate_hbm.at[0],
                                     wproj_hbm.at[0], wkv_hbm),
                                    (poolw_ref, wgu_ref, wdown_ref, wgate_ref, wproj_ref, wkv_ref)),
                     stage_ref, sem)

    @pl.when(step == 0)
    def _():
        actb_ref[...] = jnp.zeros(actb_ref.shape, BF16)
        x1b_ref[...] = jnp.zeros(x1b_ref.shape, F32)

    gains = gains_ref[...]
    gain = lambda r: gains[r:r + 1, :]
    seq_tile = jnp.minimum(step, n_tiles - 1) % tiles_per_seq

    def finish_b():
        x1 = x1b_ref[...]
        f = yield from _ffn_down(actb_ref, wdown_ref)
        x3 = yield from _embed(x1, f, pb_ref[...], gain, wgate_ref, wproj_ref)
        yield from _emit_layer0(x3, gain, wkv_ref, xob_ref, kb_ref, vb_ref)

    def chain_a():
        x = x_ref[0:HALF_ROWS, :]
        h_hist = jnp.where(seq_tile == 0, 0.0, _rms(xhalo_ref[...], gain(G_PRE_MIX)))
        x1 = yield from _pool_mixer(x, h_hist, seq_tile * TILE_ROWS, gain, hext_ref.at[0], poolw_ref)
        yield from _ffn_up(x1, gain, wgu_ref, acta_ref)
        f = yield from _ffn_down(acta_ref, wdown_ref)
        x3 = yield from _embed(x1, f, pa_ref[...], gain, wgate_ref, wproj_ref)
        yield from _emit_layer0(x3, gain, wkv_ref, xoa_ref, ka_ref, va_ref)

    def start_b():
        x = x_ref[HALF_ROWS:, :]
        h_hist = _rms(x_ref[HALF_ROWS - POOL_HALO:HALF_ROWS, :], gain(G_PRE_MIX))
        x1 = yield from _pool_mixer(x, h_hist, seq_tile * TILE_ROWS + HALF_ROWS, gain, hext_ref.at[1],
                                    poolw_ref)
        x1b_ref[...] = x1
        yield from _ffn_up(x1, gain, wgu_ref, actb_ref)

    @pl.when(step < n_tiles)
    def _():
        _run_staggered([finish_b(), chain_a(), start_b()], lead=[0, L0_CHAIN_PIECES - L0_START_B_PIECES])

    @pl.when(step == n_tiles)
    def _():
        _run_staggered([finish_b()], lead=[])


def _layer1_chain(tile, gain, xa_ref, xb_ref, ka_ref, kb_ref, khalo_ref, va_ref, vb_ref, vhalo_ref, p_ref,
                  bias_ref, sink_ref, xo_ref, attn_ref, act_ref, wq_ref, wo_ref, wgu_ref, wdown_ref,
                  wgate_ref, wproj_ref):
    x = jnp.concatenate([xa_ref[...], xb_ref[...]], axis=0)
    hq = _rms(x, gain(G_PRE_MIX)).astype(BF16)
    yield
    q_pieces = []
    for lo, hi in _col_pieces(D_MODEL):
        q_pieces.append(_mm(hq, wq_ref[:, lo:hi]) * (HEAD_DIM ** -0.5))
        yield
    q = jnp.concatenate(q_pieces, axis=-1)

    blocks_per_half = HALF_ROWS // ATTN_BLOCK

    def kv_block(a_ref, b_ref, halo_ref, blk, c0):
        if blk < 0:
            return halo_ref[:, c0:c0 + LANES]
        ref = a_ref if blk < blocks_per_half else b_ref
        r = (blk % blocks_per_half) * ATTN_BLOCK
        return ref[r:r + ATTN_BLOCK, c0:c0 + LANES]

    left = lax.broadcasted_iota(jnp.int32, (ATTN_BLOCK, LANES), 1) < HEAD_DIM
    for j in range(TILE_ROWS // ATTN_BLOCK):
        r0 = j * ATTN_BLOCK
        for kt in range(KV_TILES):
            c0 = kt * LANES
            k_win = jnp.concatenate([kv_block(ka_ref, kb_ref, khalo_ref, j - 1, c0),
                                     kv_block(ka_ref, kb_ref, khalo_ref, j, c0)], axis=0)
            v_win = jnp.concatenate([kv_block(va_ref, vb_ref, vhalo_ref, j - 1, c0),
                                     kv_block(va_ref, vb_ref, vhalo_ref, j, c0)], axis=0)
            q_tiles = [q[r0:r0 + ATTN_BLOCK, (kt * GQA_GROUP + g) * LANES:(kt * GQA_GROUP + g + 1) * LANES]
                       for g in range(GQA_GROUP)]
            q8 = jnp.concatenate([jnp.where(left, t, 0.0) for t in q_tiles]
                                 + [jnp.where(left, 0.0, t) for t in q_tiles], axis=0).astype(BF16)
            s = lax.dot_general(q8, k_win, (((1,), (1,)), ((), ())), preferred_element_type=F32)
            s = s + bias_ref[kt]
            s_prev, s_cur = s[:, :ATTN_BLOCK], s[:, ATTN_BLOCK:]
            if j == 0:
                s_prev = s_prev + jnp.where(tile == 0, NEG_INF, 0.0)
            sink = sink_ref[kt]
            m = jnp.max(jnp.maximum(jnp.maximum(s_prev, s_cur), sink), axis=-1, keepdims=True)
            p_prev, p_cur = jnp.exp(s_prev - m), jnp.exp(s_cur - m)
            denom = jnp.sum(p_prev + p_cur, axis=-1, keepdims=True) + jnp.exp(sink - m)
            probs = jnp.concatenate([p_prev, p_cur], axis=-1).astype(BF16)
            o = _mm(probs, v_win) / denom
            for g in range(GQA_GROUP):
                a = o[g * ATTN_BLOCK:(g + 1) * ATTN_BLOCK]
                b = o[(GQA_GROUP + g) * ATTN_BLOCK:(GQA_GROUP + g + 1) * ATTN_BLOCK]
                tile_lo = (kt * GQA_GROUP + g) * LANES
                attn_ref[r0:r0 + ATTN_BLOCK, tile_lo:tile_lo + LANES] = jnp.where(left, a, b).astype(BF16)
            yield

    y = []
    for lo, hi in _col_pieces(D_MODEL):
        y.append(_mm(attn_ref[...], wo_ref[:, lo:hi]))
        yield
    x1 = x + _rms(jnp.concatenate(y, axis=-1), gain(G_POST_MIX))
    yield
    yield from _ffn_up(x1, gain, wgu_ref, act_ref)
    f = yield from _ffn_down(act_ref, wdown_ref)
    xo_ref[...] = yield from _embed(x1, f, p_ref[...], gain, wgate_ref, wproj_ref)


def _layer1_body(xa_ref, xb_ref, ka_ref, kb_ref, khalo_ref, va_ref, vb_ref, vhalo_ref, p_ref, gains_ref,
                 bias_ref, sink_ref, wq_hbm, wo_hbm, wgu_hbm, wdown_hbm, wgate_hbm, wproj_hbm, xo_ref,
                 attn_ref, act_ref, wq_ref, wo_ref, wgu_ref, wdown_ref, wgate_ref, wproj_ref, stage_ref, sem):
    weights = (wq_ref, wo_ref, wgu_ref, wdown_ref, wgate_ref, wproj_ref)
    _convert_weights((pl.program_id(0) == 0) & (pl.program_id(1) == 0),
                     zip((wq_hbm, wo_hbm, wgu_hbm.at[1], wdown_hbm.at[1], wgate_hbm.at[1], wproj_hbm.at[1]),
                         weights), stage_ref, sem)
    tile = pl.program_id(1)
    gains = gains_ref[...]
    gain = lambda r: gains[r:r + 1, :]
    _run_staggered([_layer1_chain(tile, gain, xa_ref, xb_ref, ka_ref, kb_ref, khalo_ref, va_ref, vb_ref,
                                  vhalo_ref, p_ref, bias_ref, sink_ref, xo_ref, attn_ref, act_ref, *weights)],
                   lead=[])


def _head_order():
    order = []
    for kt in range(KV_TILES):
        for g in range(GQA_GROUP):
            order += [(2 * kt) * GQA_GROUP + g, (2 * kt + 1) * GQA_GROUP + g]
    return order


def _head_columns():
    return jnp.asarray([h * HEAD_DIM + c for h in _head_order() for c in range(HEAD_DIM)], jnp.int32)


def _attention_tables(sinks):
    qi = jnp.arange(ATTN_BLOCK)[:, None]
    si = jnp.arange(2 * ATTN_BLOCK)[None, :]
    rel = ATTN_BLOCK + qi - si
    valid = (rel >= 0) & (rel < ATTN_BLOCK)
    heads = jnp.arange(1, N_HEADS + 1, dtype=F32)
    slopes = jnp.exp2(-8.0 * heads / N_HEADS).reshape(N_KV_HEADS, GQA_GROUP)
    bias = jnp.where(valid[None, None], -slopes[:, :, None, None] * rel.astype(F32)[None, None], NEG_INF)
    bias = bias.reshape(KV_TILES, HEADS_PER_KV_TILE * ATTN_BLOCK, 2 * ATTN_BLOCK)
    sink = sinks.astype(F32).reshape(N_KV_HEADS, GQA_GROUP, 1, 1)
    sink = jnp.broadcast_to(sink, (N_KV_HEADS, GQA_GROUP, ATTN_BLOCK, LANES))
    return bias, sink.reshape(KV_TILES, HEADS_PER_KV_TILE * ATTN_BLOCK, LANES)


def kernel(x, p, pre_mix_g, post_mix_g, pre_ffn_g, post_ffn_g, pool_w, pool_scale, kv_g, w_kv, w_q, sinks,
           w_o, w_gu, w_down, ple_g, w_ple_gate, w_ple_proj, ple_post_g):
    batch, seq, _ = x.shape
    tm, hm = TILE_ROWS, HALF_ROWS
    assert seq % tm == 0 and hm % ATTN_BLOCK == 0 and hm % POOL_HALO == 0
    tokens = batch * seq
    n_tiles = tokens // tm
    tiles_per_seq = seq // tm

    def gains(i, extra_a, extra_b):
        return jnp.stack([pre_mix_g[i], post_mix_g[i], pre_ffn_g[i], post_ffn_g[i], ple_g[i],
                          ple_post_g[i], extra_a, extra_b]).astype(F32)

    zeros = jnp.zeros((D_MODEL,), F32)
    gains0 = gains(0, pool_scale[0], kv_g)
    gains1 = gains(1, zeros, zeros)
    head_cols = _head_columns()
    bias, sink = _attention_tables(sinks[0])
    pool_w_rows = pool_w.reshape(pool_w.shape[0], D_MODEL, POOL_GROUP)
    x_rows = x.reshape(tokens, D_MODEL)
    p_rows = p.reshape(p.shape[0], tokens, PLE_DIM)

    in_hbm = pl.BlockSpec(memory_space=pl.ANY)
    bf16_weight = lambda rows, cols: pltpu.VMEM((rows, cols), BF16)
    ffn_weights = [bf16_weight(D_MODEL, 2 * D_FF), bf16_weight(D_FF, D_MODEL), bf16_weight(D_MODEL, D_MODEL),
                   bf16_weight(PLE_DIM, D_MODEL)]
    staging = [pltpu.VMEM((2, STAGE_ROWS, STAGE_COLS), F32), pltpu.SemaphoreType.DMA((2,))]

    this_tile = lambda s: jnp.minimum(s, n_tiles - 1)
    prev_tile = lambda s: jnp.maximum(s - 1, 0)
    whole = lambda shape: pl.BlockSpec(shape, lambda s: (0,) * len(shape), pipeline_mode=pl.Buffered(1))
    half_a = lambda width: pl.BlockSpec((hm, width), lambda s: (this_tile(s), 0))
    half_b = lambda width: pl.BlockSpec((hm, width), lambda s: (prev_tile(s), 0))
    half_out = lambda width, dtype: jax.ShapeDtypeStruct((tokens // 2, width), dtype)
    x3a, ka, va, x3b, kb, vb = pl.pallas_call(
        functools.partial(_layer0_body, tiles_per_seq=tiles_per_seq, n_tiles=n_tiles),
        grid=(n_tiles + 1,),
        in_specs=[pl.BlockSpec((tm, D_MODEL), lambda s: (this_tile(s), 0)),
                  pl.BlockSpec((POOL_HALO, D_MODEL),
                               lambda s: (jnp.maximum(this_tile(s) * (tm // POOL_HALO) - 1, 0), 0)),
                  pl.BlockSpec((None, hm, PLE_DIM), lambda s: (0, 2 * this_tile(s), 0)),
                  pl.BlockSpec((None, hm, PLE_DIM), lambda s: (0, 2 * prev_tile(s) + 1, 0)),
                  whole((8, D_MODEL))] + [in_hbm] * 6,
        out_specs=[half_a(D_MODEL), half_a(KV_WIDTH), half_a(KV_WIDTH),
                   half_b(D_MODEL), half_b(KV_WIDTH), half_b(KV_WIDTH)],
        out_shape=[half_out(D_MODEL, F32), half_out(KV_WIDTH, BF16), half_out(KV_WIDTH, BF16)] * 2,
        scratch_shapes=[pltpu.VMEM((2, hm + POOL_HALO, D_MODEL), F32), pltpu.VMEM((hm, D_FF), BF16),
                        pltpu.VMEM((hm, D_FF), BF16), pltpu.VMEM((hm, D_MODEL), F32),
                        bf16_weight(D_MODEL, POOL_GROUP)] + ffn_weights
                       + [bf16_weight(D_MODEL, 2 * KV_WIDTH)] + staging,
        compiler_params=pltpu.CompilerParams(dimension_semantics=("arbitrary",),
                                             vmem_limit_bytes=VMEM_LIMIT_BYTES),
        name="yoco_layer0",
    )(x_rows, x_rows, p_rows, p_rows, gains0, pool_w_rows, w_gu, w_down, w_ple_gate, w_ple_proj, w_kv)

    halves = lambda a: a.reshape(batch, seq // 2, a.shape[-1])
    x3a, ka, va, x3b, kb, vb = map(halves, (x3a, ka, va, x3b, kb, vb))
    resident = lambda shape: pl.BlockSpec(shape, lambda b, i: (0,) * len(shape), pipeline_mode=pl.Buffered(1))
    half_tile = lambda width: pl.BlockSpec((None, hm, width), lambda b, i: (b, i, 0))
    kv_halo = pl.BlockSpec((None, ATTN_BLOCK, KV_WIDTH),
                           lambda b, i: (b, jnp.maximum(i * (hm // ATTN_BLOCK) - 1, 0), 0))
    table = lambda width: resident((KV_TILES, HEADS_PER_KV_TILE * ATTN_BLOCK, width))
    return pl.pallas_call(
        _layer1_body,
        grid=(batch, tiles_per_seq),
        in_specs=[half_tile(D_MODEL), half_tile(D_MODEL), half_tile(KV_WIDTH), half_tile(KV_WIDTH), kv_halo,
                  half_tile(KV_WIDTH), half_tile(KV_WIDTH), kv_halo,
                  pl.BlockSpec((None, None, tm, PLE_DIM), lambda b, i: (1, b, i, 0)),
                  resident((8, D_MODEL)), table(2 * ATTN_BLOCK), table(LANES)] + [in_hbm] * 6,
        out_specs=pl.BlockSpec((None, tm, D_MODEL), lambda b, i: (b, i, 0)),
        out_shape=jax.ShapeDtypeStruct((batch, seq, D_MODEL), F32),
        scratch_shapes=[pltpu.VMEM((tm, D_MODEL), BF16), pltpu.VMEM((tm, D_FF), BF16),
                        bf16_weight(D_MODEL, D_MODEL), bf16_weight(D_MODEL, D_MODEL)] + ffn_weights + staging,
        compiler_params=pltpu.CompilerParams(dimension_semantics=("arbitrary", "arbitrary"),
                                             vmem_limit_bytes=VMEM_LIMIT_BYTES),
        name="yoco_layer1",
    )(x3a, x3b, ka, kb, kb, va, vb, vb, p, gains1, bias, sink, w_q[0][:, head_cols], w_o[0][head_cols, :],
      w_gu, w_down, w_ple_gate, w_ple_proj)
```

```python
import functools

import jax
import jax.numpy as jnp
from jax import lax
from jax.experimental import pallas as pl
from jax.experimental.pallas import tpu as pltpu

D_MODEL = 1024
POOL_WINDOWS = (2, 4, 8, 16)
POOL_GROUP = D_MODEL // len(POOL_WINDOWS)
POOL_HALO = 16
HEAD_DIM = 64
N_HEADS = D_MODEL // HEAD_DIM
N_KV_HEADS = 4
GQA_GROUP = N_HEADS // N_KV_HEADS
ATTN_BLOCK = 128
D_FF = 2816
PLE_DIM = 256
EPS = 1e-6
NEG_INF = -1e30
LANES = 128
KV_WIDTH = N_KV_HEADS * HEAD_DIM
KV_TILES = KV_WIDTH // LANES
HEADS_PER_KV_TILE = 2 * GQA_GROUP

TILE_ROWS = 512
HALF_ROWS = TILE_ROWS // 2
FF_CHUNK = 256
COL_PIECE = 256
STAGE_ROWS, STAGE_COLS = 1024, 512
VMEM_LIMIT_BYTES = 56 * 1024 * 1024

BF16 = jnp.bfloat16
F32 = jnp.float32

G_PRE_MIX, G_POST_MIX, G_PRE_FFN, G_POST_FFN, G_PLE, G_PLE_POST, G_POOL_SCALE, G_KV = range(8)


def _rms(x, g):
    return x * lax.rsqrt(jnp.mean(x * x, axis=-1, keepdims=True) + EPS) * g


def _sigmoid(x):
    return 1.0 / (1.0 + jnp.exp(-x))


def _mm(a, b):
    return jnp.dot(a, b, preferred_element_type=F32)


def _weight_chunks(src, dst):
    rows, cols = dst.shape
    chunks = []
    for r0 in range(0, rows, STAGE_ROWS):
        nr = min(STAGE_ROWS, rows - r0)
        for c0 in range(0, cols, STAGE_COLS):
            nc = min(STAGE_COLS, cols - c0)
            window = (pl.ds(r0, nr), pl.ds(c0, nc))
            chunks.append((src.at[window], dst.at[window], nr, nc))
    return chunks


def _convert_weights(first_step, pairs, stage_ref, sem):
    @pl.when(first_step)
    def _():
        chunks = [c for src, dst in pairs for c in _weight_chunks(src, dst)]

        def copy(i):
            src, _, nr, nc = chunks[i]
            return pltpu.make_async_copy(src, stage_ref.at[i % 2, pl.ds(0, nr), pl.ds(0, nc)], sem.at[i % 2])

        copy(0).start()
        for i, (_, dst, nr, nc) in enumerate(chunks):
            if i + 1 < len(chunks):
                copy(i + 1).start()
            copy(i).wait()
            dst[...] = stage_ref[i % 2, 0:nr, 0:nc].astype(BF16)


def _run_staggered(chains, lead):
    chains = list(chains)
    count, done = [0] * len(chains), [False] * len(chains)
    while not all(done):
        for i, chain in enumerate(chains):
            if i > 0 and count[i - 1] < lead[i - 1] and not done[i - 1]:
                break
            if done[i]:
                continue
            try:
                next(chain)
                count[i] += 1
            except StopIteration:
                done[i] = True


def _col_pieces(width):
    return [(lo, lo + COL_PIECE) for lo in range(0, width, COL_PIECE)]


def _ffn_up(x1, gain, wgu_ref, act_ref):
    h = _rms(x1, gain(G_PRE_FFN)).astype(BF16)
    for lo in range(0, D_FF, FF_CHUNK):
        gate = _mm(h, wgu_ref[:, lo:lo + FF_CHUNK])
        up = _mm(h, wgu_ref[:, D_FF + lo:D_FF + lo + FF_CHUNK])
        act_ref[:, lo:lo + FF_CHUNK] = (gate * _sigmoid(gate) * up).astype(BF16)
        yield


def _ffn_down(act_ref, wdown_ref):
    f = []
    for lo, hi in _col_pieces(D_MODEL):
        f.append(_mm(act_ref[...], wdown_ref[:, lo:hi]))
        yield
    return jnp.concatenate(f, axis=-1)


def _embed(x1, f, p, gain, wgate_ref, wproj_ref):
    x2 = x1 + _rms(f, gain(G_POST_FFN))
    hg = _rms(x2, gain(G_PLE)).astype(BF16)
    pb = p.astype(BF16)
    yield
    e = []
    for lo, hi in _col_pieces(D_MODEL):
        e.append(_mm(pb, wproj_ref[:, lo:hi]) * _sigmoid(_mm(hg, wgate_ref[:, lo:hi])))
        yield
    return x2 + _rms(jnp.concatenate(e, axis=-1), gain(G_PLE_POST))


L0_MIX_PIECES = 2 + len(POOL_WINDOWS)
L0_START_B_PIECES = L0_MIX_PIECES + D_FF // FF_CHUNK
L0_CHAIN_PIECES = L0_START_B_PIECES + 2 * (D_MODEL // COL_PIECE) + 1 + 3


def _pool_mixer(x, h_hist, pos0, gain, hext, poolw_ref):
    n = x.shape[0]
    h = _rms(x, gain(G_PRE_MIX))
    hext[0:POOL_HALO, :] = h_hist
    hext[POOL_HALO:POOL_HALO + n, :] = h
    yield
    pos1 = lax.broadcasted_iota(jnp.int32, (n, POOL_GROUP), 0) + (pos0 + 1)
    mixed = []
    for gi, w in enumerate(POOL_WINDOWS):
        lo = gi * POOL_GROUP
        terms = [hext[POOL_HALO - j:POOL_HALO - j + n, lo:lo + POOL_GROUP] for j in range(w)]
        while len(terms) > 1:
            terms = [terms[a] + terms[a + 1] for a in range(0, len(terms), 2)]
        cnt = jnp.minimum(pos1, w).astype(F32)
        d = terms[0] / cnt - h[:, lo:lo + POOL_GROUP]
        mixed.append(_mm(d.astype(BF16), poolw_ref[lo:lo + POOL_GROUP, :]))
        yield
    y = jnp.concatenate(mixed, axis=-1) * gain(G_POOL_SCALE)
    x1 = x + _rms(y, gain(G_POST_MIX))
    yield
    return x1


def _emit_layer0(x3, gain, wkv_ref, xo_ref, k_ref, v_ref):
    xo_ref[...] = x3
    hkv = _rms(x3, gain(G_KV)).astype(BF16)
    yield
    k_ref[...] = _mm(hkv, wkv_ref[:, :KV_WIDTH]).astype(BF16)
    yield
    v_ref[...] = _mm(hkv, wkv_ref[:, KV_WIDTH:]).astype(BF16)


def _layer0_body(x_ref, xhalo_ref, pa_ref, pb_ref, gains_ref, poolw_hbm, wgu_hbm, wdown_hbm, wgate_hbm,
                 wproj_hbm, wkv_hbm, xoa_ref, ka_ref, va_ref, xob_ref, kb_ref, vb_ref,
                 hext_ref, acta_ref, actb_ref, x1b_ref, poolw_ref, wgu_ref, wdown_ref, wgate_ref, wproj_ref,
                 wkv_ref, stage_ref, sem, *, tiles_per_seq, n_tiles):
    step = pl.program_id(0)
    _convert_weights(step == 0, zip((poolw_hbm.at[0], wgu_hbm.at[0], wdown_hbm.at[0], wgate_hbm.at[0],
                                     wproj_hbm.at[0], wkv_hbm),
                                    (poolw_ref, wgu_ref, wdown_ref, wgate_ref, wproj_ref, wkv_ref)),
                     stage_ref, sem)

    @pl.when(step == 0)
    def _():
        actb_ref[...] = jnp.zeros(actb_ref.shape, BF16)
        x1b_ref[...] = jnp.zeros(x1b_ref.shape, F32)

    gains = gains_ref[...]
    gain = lambda r: gains[r:r + 1, :]
    seq_tile = jnp.minimum(step, n_tiles - 1) % tiles_per_seq

    def finish_b():
        x1 = x1b_ref[...]
        f = yield from _ffn_down(actb_ref, wdown_ref)
        x3 = yield from _embed(x1, f, pb_ref[...], gain, wgate_ref, wproj_ref)
        yield from _emit_layer0(x3, gain, wkv_ref, xob_ref, kb_ref, vb_ref)

    def chain_a():
        x = x_ref[0:HALF_ROWS, :]
        h_hist = jnp.where(seq_tile == 0, 0.0, _rms(xhalo_ref[...], gain(G_PRE_MIX)))
        x1 = yield from _pool_mixer(x, h_hist, seq_tile * TILE_ROWS, gain, hext_ref.at[0], poolw_ref)
        yield from _ffn_up(x1, gain, wgu_ref, acta_ref)
        f = yield from _ffn_down(acta_ref, wdown_ref)
        x3 = yield from _embed(x1, f, pa_ref[...], gain, wgate_ref, wproj_ref)
        yield from _emit_layer0(x3, gain, wkv_ref, xoa_ref, ka_ref, va_ref)

    def start_b():
        x = x_ref[HALF_ROWS:, :]
        h_hist = _rms(x_ref[HALF_ROWS - POOL_HALO:HALF_ROWS, :], gain(G_PRE_MIX))
        x1 = yield from _pool_mixer(x, h_hist, seq_tile * TILE_ROWS + HALF_ROWS, gain, hext_ref.at[1],
                                    poolw_ref)
        x1b_ref[...] = x1
        yield from _ffn_up(x1, gain, wgu_ref, actb_ref)

    @pl.when(step < n_tiles)
    def _():
        _run_staggered([finish_b(), chain_a(), start_b()], lead=[0, L0_CHAIN_PIECES - L0_START_B_PIECES])

    @pl.when(step == n_tiles)
    def _():
        _run_staggered([finish_b()], lead=[])


L1_MIX_PIECES = 2 + 2 * (D_MODEL // COL_PIECE) + (HALF_ROWS // ATTN_BLOCK) * KV_TILES
L1_START_B_PIECES = L1_MIX_PIECES + D_FF // FF_CHUNK
L1_CHAIN_PIECES = L1_START_B_PIECES + 2 * (D_MODEL // COL_PIECE) + 1


def _attention_mixer(x, k_block, v_block, no_prev, gain, bias_ref, sink_ref, attn_ref, wq_ref, wo_ref):
    n = x.shape[0]
    hq = _rms(x, gain(G_PRE_MIX)).astype(BF16)
    yield
    q_pieces = []
    for lo, hi in _col_pieces(D_MODEL):
        q_pieces.append(_mm(hq, wq_ref[:, lo:hi]) * (HEAD_DIM ** -0.5))
        yield
    q = jnp.concatenate(q_pieces, axis=-1)

    left = lax.broadcasted_iota(jnp.int32, (ATTN_BLOCK, LANES), 1) < HEAD_DIM
    for j in range(n // ATTN_BLOCK):
        r0 = j * ATTN_BLOCK
        for kt in range(KV_TILES):
            c0 = kt * LANES
            k_win = jnp.concatenate([k_block(j - 1, c0), k_block(j, c0)], axis=0)
            v_win = jnp.concatenate([v_block(j - 1, c0), v_block(j, c0)], axis=0)
            q_tiles = [q[r0:r0 + ATTN_BLOCK, (kt * GQA_GROUP + g) * LANES:(kt * GQA_GROUP + g + 1) * LANES]
                       for g in range(GQA_GROUP)]
            q8 = jnp.concatenate([jnp.where(left, t, 0.0) for t in q_tiles]
                                 + [jnp.where(left, 0.0, t) for t in q_tiles], axis=0).astype(BF16)
            s = lax.dot_general(q8, k_win, (((1,), (1,)), ((), ())), preferred_element_type=F32)
            s = s + bias_ref[kt]
            s_prev, s_cur = s[:, :ATTN_BLOCK], s[:, ATTN_BLOCK:]
            if j == 0 and no_prev is not None:
                s_prev = s_prev + jnp.where(no_prev, NEG_INF, 0.0)
            sink = sink_ref[kt]
            m = jnp.max(jnp.maximum(jnp.maximum(s_prev, s_cur), sink), axis=-1, keepdims=True)
            p_prev, p_cur = jnp.exp(s_prev - m), jnp.exp(s_cur - m)
            denom = jnp.sum(p_prev + p_cur, axis=-1, keepdims=True) + jnp.exp(sink - m)
            probs = jnp.concatenate([p_prev, p_cur], axis=-1).astype(BF16)
            o = _mm(probs, v_win) / denom
            for g in range(GQA_GROUP):
                a = o[g * ATTN_BLOCK:(g + 1) * ATTN_BLOCK]
                b = o[(GQA_GROUP + g) * ATTN_BLOCK:(GQA_GROUP + g + 1) * ATTN_BLOCK]
                tile_lo = (kt * GQA_GROUP + g) * LANES
                attn_ref[r0:r0 + ATTN_BLOCK, tile_lo:tile_lo + LANES] = jnp.where(left, a, b).astype(BF16)
            yield

    y = []
    for lo, hi in _col_pieces(D_MODEL):
        y.append(_mm(attn_ref[...], wo_ref[:, lo:hi]))
        yield
    x1 = x + _rms(jnp.concatenate(y, axis=-1), gain(G_POST_MIX))
    yield
    return x1


def _blocks_of(cur_ref, before_ref):
    def get(blk, c0):
        if blk < 0:
            return before_ref[:, c0:c0 + LANES]
        return cur_ref[blk * ATTN_BLOCK:(blk + 1) * ATTN_BLOCK, c0:c0 + LANES]
    return get


def _layer1_body(xa_ref, xb_ref, ka_ref, khalo_ref, va_ref, vhalo_ref, kan_ref, kbn_ref, van_ref, vbn_ref,
                 pa_ref, pb_ref, gains_ref, bias_ref, sink_ref, wq_hbm, wo_hbm, wgu_hbm, wdown_hbm, wgate_hbm,
                 wproj_hbm, xo_ref, attna_ref, attnb_ref, acta_ref, actb_ref, x1b_ref, wq_ref, wo_ref, wgu_ref,
                 wdown_ref, wgate_ref, wproj_ref, stage_ref, sem, *, tiles_per_seq):
    step = pl.program_id(0)
    _convert_weights(step == 0, zip((wq_hbm, wo_hbm, wgu_hbm.at[1], wdown_hbm.at[1], wgate_hbm.at[1],
                                     wproj_hbm.at[1]),
                                    (wq_ref, wo_ref, wgu_ref, wdown_ref, wgate_ref, wproj_ref)), stage_ref, sem)
    gains = gains_ref[...]
    gain = lambda r: gains[r:r + 1, :]
    tile_is_seq_start = (jnp.maximum(step - 1, 0) % tiles_per_seq) == 0

    def finish_b():
        x1 = x1b_ref[...]
        f = yield from _ffn_down(actb_ref, wdown_ref)
        xo_ref[HALF_ROWS:, :] = yield from _embed(x1, f, pb_ref[...], gain, wgate_ref, wproj_ref)

    def chain_a():
        x1 = yield from _attention_mixer(xa_ref[...], _blocks_of(ka_ref, khalo_ref), _blocks_of(va_ref, vhalo_ref),
                                         tile_is_seq_start, gain, bias_ref, sink_ref, attna_ref, wq_ref, wo_ref)
        yield from _ffn_up(x1, gain, wgu_ref, acta_ref)
        f = yield from _ffn_down(acta_ref, wdown_ref)
        xo_ref[0:HALF_ROWS, :] = yield from _embed(x1, f, pa_ref[...], gain, wgate_ref, wproj_ref)

    def start_b():
        x1 = yield from _attention_mixer(xb_ref[...], _blocks_of(kbn_ref, kan_ref), _blocks_of(vbn_ref, van_ref),
                                         None, gain, bias_ref, sink_ref, attnb_ref, wq_ref, wo_ref)
        x1b_ref[...] = x1
        yield from _ffn_up(x1, gain, wgu_ref, actb_ref)

    @pl.when(step == 0)
    def _():
        _run_staggered([start_b()], lead=[])

    @pl.when(step > 0)
    def _():
        _run_staggered([finish_b(), chain_a(), start_b()], lead=[0, L1_CHAIN_PIECES - L1_START_B_PIECES])


def _head_order():
    order = []
    for kt in range(KV_TILES):
        for g in range(GQA_GROUP):
            order += [(2 * kt) * GQA_GROUP + g, (2 * kt + 1) * GQA_GROUP + g]
    return order


def _head_columns():
    return jnp.asarray([h * HEAD_DIM + c for h in _head_order() for c in range(HEAD_DIM)], jnp.int32)


def _attention_tables(sinks):
    qi = jnp.arange(ATTN_BLOCK)[:, None]
    si = jnp.arange(2 * ATTN_BLOCK)[None, :]
    rel = ATTN_BLOCK + qi - si
    valid = (rel >= 0) & (rel < ATTN_BLOCK)
    heads = jnp.arange(1, N_HEADS + 1, dtype=F32)
    slopes = jnp.exp2(-8.0 * heads / N_HEADS).reshape(N_KV_HEADS, GQA_GROUP)
    bias = jnp.where(valid[None, None], -slopes[:, :, None, None] * rel.astype(F32)[None, None], NEG_INF)
    bias = bias.reshape(KV_TILES, HEADS_PER_KV_TILE * ATTN_BLOCK, 2 * ATTN_BLOCK)
    sink = sinks.astype(F32).reshape(N_KV_HEADS, GQA_GROUP, 1, 1)
    sink = jnp.broadcast_to(sink, (N_KV_HEADS, GQA_GROUP, ATTN_BLOCK, LANES))
    return bias, sink.reshape(KV_TILES, HEADS_PER_KV_TILE * ATTN_BLOCK, LANES)


def kernel(x, p, pre_mix_g, post_mix_g, pre_ffn_g, post_ffn_g, pool_w, pool_scale, kv_g, w_kv, w_q, sinks,
           w_o, w_gu, w_down, ple_g, w_ple_gate, w_ple_proj, ple_post_g):
    batch, seq, _ = x.shape
    tm, hm = TILE_ROWS, HALF_ROWS
    assert seq % tm == 0 and hm % ATTN_BLOCK == 0 and hm % POOL_HALO == 0
    tokens = batch * seq
    n_tiles = tokens // tm
    tiles_per_seq = seq // tm

    def gains(i, extra_a, extra_b):
        return jnp.stack([pre_mix_g[i], post_mix_g[i], pre_ffn_g[i], post_ffn_g[i], ple_g[i],
                          ple_post_g[i], extra_a, extra_b]).astype(F32)

    zeros = jnp.zeros((D_MODEL,), F32)
    gains0 = gains(0, pool_scale[0], kv_g)
    gains1 = gains(1, zeros, zeros)
    head_cols = _head_columns()
    bias, sink = _attention_tables(sinks[0])
    pool_w_rows = pool_w.reshape(pool_w.shape[0], D_MODEL, POOL_GROUP)
    x_rows = x.reshape(tokens, D_MODEL)
    p_rows = p.reshape(p.shape[0], tokens, PLE_DIM)

    in_hbm = pl.BlockSpec(memory_space=pl.ANY)
    bf16_weight = lambda rows, cols: pltpu.VMEM((rows, cols), BF16)
    ffn_weights = [bf16_weight(D_MODEL, 2 * D_FF), bf16_weight(D_FF, D_MODEL), bf16_weight(D_MODEL, D_MODEL),
                   bf16_weight(PLE_DIM, D_MODEL)]
    staging = [pltpu.VMEM((2, STAGE_ROWS, STAGE_COLS), F32), pltpu.SemaphoreType.DMA((2,))]

    this_tile = lambda s: jnp.minimum(s, n_tiles - 1)
    prev_tile = lambda s: jnp.maximum(s - 1, 0)
    whole = lambda shape: pl.BlockSpec(shape, lambda s: (0,) * len(shape), pipeline_mode=pl.Buffered(1))
    half_a = lambda width: pl.BlockSpec((hm, width), lambda s: (this_tile(s), 0))
    half_b = lambda width: pl.BlockSpec((hm, width), lambda s: (prev_tile(s), 0))
    half_out = lambda width, dtype: jax.ShapeDtypeStruct((tokens // 2, width), dtype)
    x3a, ka, va, x3b, kb, vb = pl.pallas_call(
        functools.partial(_layer0_body, tiles_per_seq=tiles_per_seq, n_tiles=n_tiles),
        grid=(n_tiles + 1,),
        in_specs=[pl.BlockSpec((tm, D_MODEL), lambda s: (this_tile(s), 0)),
                  pl.BlockSpec((POOL_HALO, D_MODEL),
                               lambda s: (jnp.maximum(this_tile(s) * (tm // POOL_HALO) - 1, 0), 0)),
                  pl.BlockSpec((None, hm, PLE_DIM), lambda s: (0, 2 * this_tile(s), 0)),
                  pl.BlockSpec((None, hm, PLE_DIM), lambda s: (0, 2 * prev_tile(s) + 1, 0)),
                  whole((8, D_MODEL))] + [in_hbm] * 6,
        out_specs=[half_a(D_MODEL), half_a(KV_WIDTH), half_a(KV_WIDTH),
                   half_b(D_MODEL), half_b(KV_WIDTH), half_b(KV_WIDTH)],
        out_shape=[half_out(D_MODEL, F32), half_out(KV_WIDTH, BF16), half_out(KV_WIDTH, BF16)] * 2,
        scratch_shapes=[pltpu.VMEM((2, hm + POOL_HALO, D_MODEL), F32), pltpu.VMEM((hm, D_FF), BF16),
                        pltpu.VMEM((hm, D_FF), BF16), pltpu.VMEM((hm, D_MODEL), F32),
                        bf16_weight(D_MODEL, POOL_GROUP)] + ffn_weights
                       + [bf16_weight(D_MODEL, 2 * KV_WIDTH)] + staging,
        compiler_params=pltpu.CompilerParams(dimension_semantics=("arbitrary",),
                                             vmem_limit_bytes=VMEM_LIMIT_BYTES),
        name="yoco_layer0",
    )(x_rows, x_rows, p_rows, p_rows, gains0, pool_w_rows, w_gu, w_down, w_ple_gate, w_ple_proj, w_kv)

    done_tile = lambda g: jnp.maximum(g - 1, 0)
    next_tile = lambda g: jnp.minimum(g, n_tiles - 1)
    rows_of = lambda rows, width, block: pl.BlockSpec((rows, width), lambda g: (block(g), 0))
    blocks_per_half = hm // ATTN_BLOCK
    kv_done = rows_of(hm, KV_WIDTH, done_tile)
    kv_before_a = rows_of(ATTN_BLOCK, KV_WIDTH, lambda g: jnp.maximum(done_tile(g) * blocks_per_half - 1, 0))
    kv_before_b = rows_of(ATTN_BLOCK, KV_WIDTH, lambda g: next_tile(g) * blocks_per_half + blocks_per_half - 1)
    kv_next = rows_of(hm, KV_WIDTH, next_tile)
    p_half = lambda b_half: pl.BlockSpec((None, hm, PLE_DIM), lambda g: (1, 2 * done_tile(g) + b_half, 0))
    table = lambda width: whole((KV_TILES, HEADS_PER_KV_TILE * ATTN_BLOCK, width))
    out = pl.pallas_call(
        functools.partial(_layer1_body, tiles_per_seq=tiles_per_seq),
        grid=(n_tiles + 1,),
        in_specs=[rows_of(hm, D_MODEL, done_tile), rows_of(hm, D_MODEL, next_tile),
                  kv_done, kv_before_a, kv_done, kv_before_a, kv_before_b, kv_next, kv_before_b, kv_next,
                  p_half(0), p_half(1), whole((8, D_MODEL)), table(2 * ATTN_BLOCK), table(LANES)] + [in_hbm] * 6,
        out_specs=rows_of(tm, D_MODEL, done_tile),
        out_shape=jax.ShapeDtypeStruct((tokens, D_MODEL), F32),
        scratch_shapes=[pltpu.VMEM((hm, D_MODEL), BF16), pltpu.VMEM((hm, D_MODEL), BF16),
                        pltpu.VMEM((hm, D_FF), BF16), pltpu.VMEM((hm, D_FF), BF16), pltpu.VMEM((hm, D_MODEL), F32),
                        bf16_weight(D_MODEL, D_MODEL), bf16_weight(D_MODEL, D_MODEL)] + ffn_weights + staging,
        compiler_params=pltpu.CompilerParams(dimension_semantics=("arbitrary",),
                                             vmem_limit_bytes=VMEM_LIMIT_BYTES),
        name="yoco_layer1",
    )(x3a, x3b, ka, kb, va, vb, ka, kb, va, vb, p_rows, p_rows, gains1, bias, sink,
      w_q[0][:, head_cols], w_o[0][head_cols, :], w_gu, w_down, w_ple_gate, w_ple_proj)
    return out.reshape(batch, seq, D_MODEL)
```

```python
import functools

import jax
import jax.numpy as jnp
from jax import lax
from jax.experimental import pallas as pl
from jax.experimental.pallas import tpu as pltpu

D_MODEL = 1024
POOL_WINDOWS = (2, 4, 8, 16)
POOL_GROUP = D_MODEL // len(POOL_WINDOWS)
POOL_HALO = 16
HEAD_DIM = 64
N_HEADS = D_MODEL // HEAD_DIM
N_KV_HEADS = 4
GQA_GROUP = N_HEADS // N_KV_HEADS
ATTN_BLOCK = 128
D_FF = 2816
PLE_DIM = 256
EPS = 1e-6
NEG_INF = -1e30
LANES = 128
KV_WIDTH = N_KV_HEADS * HEAD_DIM
KV_TILES = KV_WIDTH // LANES
HEADS_PER_KV_TILE = 2 * GQA_GROUP

TILE_ROWS = 512
HALF_ROWS = TILE_ROWS // 2
FF_CHUNK = 256
COL_PIECE = 256
STAGE_ROWS, STAGE_COLS = 1024, 512
VMEM_LIMIT_BYTES = 56 * 1024 * 1024

BF16 = jnp.bfloat16
F32 = jnp.float32

G_PRE_MIX, G_POST_MIX, G_PRE_FFN, G_POST_FFN, G_PLE, G_PLE_POST, G_POOL_SCALE, G_KV = range(8)


def _rms(x, g):
    return x * lax.rsqrt(jnp.mean(x * x, axis=-1, keepdims=True) + EPS) * g


def _sigmoid(x):
    return 1.0 / (1.0 + jnp.exp(-x))


def _mm(a, b):
    return jnp.dot(a, b, preferred_element_type=F32)


def _gate_up_first_use(c0, nc):
    return min(c for c in range(D_FF // FF_CHUNK) for lo in (c * FF_CHUNK, D_FF + c * FF_CHUNK)
               if lo < c0 + nc and lo + FF_CHUNK > c0)


class _LazyWeights:
    def __init__(self, hbm_refs, vmem_refs, stage_ref, sem):
        self.vmem_refs, self.stage_ref, self.sem = vmem_refs, stage_ref, sem
        chunks = []
        for w, (src, dst) in enumerate(zip(hbm_refs, vmem_refs)):
            rows, cols = dst.shape
            for c0 in range(0, cols, STAGE_COLS):
                nc = min(STAGE_COLS, cols - c0)
                use = _gate_up_first_use(c0, nc) if cols == 2 * D_FF else c0
                for r0 in range(0, rows, STAGE_ROWS):
                    chunks.append(((w, use, c0, r0), src, dst, r0, min(STAGE_ROWS, rows - r0), c0, nc))
        self.chunks = sorted(chunks, key=lambda c: c[0])
        self.done = 0
        self._copy(0).start()

    def _copy(self, i):
        _, src, _, r0, nr, c0, nc = self.chunks[i]
        return pltpu.make_async_copy(src.at[pl.ds(r0, nr), pl.ds(c0, nc)],
                                     self.stage_ref.at[i % 2, pl.ds(0, nr), pl.ds(0, nc)], self.sem.at[i % 2])

    def _convert_next(self):
        i = self.done
        if i + 1 < len(self.chunks):
            self._copy(i + 1).start()
        self._copy(i).wait()
        _, _, dst, r0, nr, c0, nc = self.chunks[i]
        dst[r0:r0 + nr, c0:c0 + nc] = self.stage_ref[i % 2, 0:nr, 0:nc].astype(BF16)
        self.done += 1

    def need(self, w, rows, cols):
        last = max(i for i, (key, _, _, r0, nr, c0, nc) in enumerate(self.chunks)
                   if key[0] == w and r0 < rows[1] and r0 + nr > rows[0] and c0 < cols[1] and c0 + nc > cols[0])
        while self.done <= last:
            self._convert_next()

    def finish(self):
        while self.done < len(self.chunks):
            self._convert_next()

    def refs(self):
        return tuple(_LazyRef(self, w) for w in range(len(self.vmem_refs)))


class _LazyRef:
    def __init__(self, weights, w):
        self.weights, self.w = weights, w

    def __getitem__(self, idx):
        ref = self.weights.vmem_refs[self.w]
        (r_lo, r_hi, _), (c_lo, c_hi, _) = (sl.indices(size) for sl, size in zip(idx, ref.shape))
        self.weights.need(self.w, (r_lo, r_hi), (c_lo, c_hi))
        return ref[idx]


def _run_staggered(chains, lead):
    chains = list(chains)
    count, done = [0] * len(chains), [False] * len(chains)
    while not all(done):
        for i, chain in enumerate(chains):
            if i > 0 and count[i - 1] < lead[i - 1] and not done[i - 1]:
                break
            if done[i]:
                continue
            try:
                next(chain)
                count[i] += 1
            except StopIteration:
                done[i] = True


def _col_pieces(width):
    return [(lo, lo + COL_PIECE) for lo in range(0, width, COL_PIECE)]


def _ffn_up(x1, gain, wgu_ref, act_ref):
    h = _rms(x1, gain(G_PRE_FFN)).astype(BF16)
    for lo in range(0, D_FF, FF_CHUNK):
        gate = _mm(h, wgu_ref[:, lo:lo + FF_CHUNK])
        up = _mm(h, wgu_ref[:, D_FF + lo:D_FF + lo + FF_CHUNK])
        act_ref[:, lo:lo + FF_CHUNK] = (gate * _sigmoid(gate) * up).astype(BF16)
        yield


def _ffn_down(act_ref, wdown_ref):
    f = []
    for lo, hi in _col_pieces(D_MODEL):
        f.append(_mm(act_ref[...], wdown_ref[:, lo:hi]))
        yield
    return jnp.concatenate(f, axis=-1)


def _embed(x1, f, p, gain, wgate_ref, wproj_ref):
    x2 = x1 + _rms(f, gain(G_POST_FFN))
    hg = _rms(x2, gain(G_PLE)).astype(BF16)
    pb = p.astype(BF16)
    yield
    e = []
    for lo, hi in _col_pieces(D_MODEL):
        e.append(_mm(pb, wproj_ref[:, lo:hi]) * _sigmoid(_mm(hg, wgate_ref[:, lo:hi])))
        yield
    return x2 + _rms(jnp.concatenate(e, axis=-1), gain(G_PLE_POST))


L0_MIX_PIECES = 2 + len(POOL_WINDOWS)
L0_START_B_PIECES = L0_MIX_PIECES + D_FF // FF_CHUNK
L0_CHAIN_PIECES = L0_START_B_PIECES + 2 * (D_MODEL // COL_PIECE) + 1 + 3


def _pool_mixer(x, h_hist, pos0, gain, hext, poolw_ref):
    n = x.shape[0]
    h = _rms(x, gain(G_PRE_MIX))
    hext[0:POOL_HALO, :] = h_hist
    hext[POOL_HALO:POOL_HALO + n, :] = h
    yield
    pos1 = lax.broadcasted_iota(jnp.int32, (n, POOL_GROUP), 0) + (pos0 + 1)
    mixed = []
    for gi, w in enumerate(POOL_WINDOWS):
        lo = gi * POOL_GROUP
        terms = [hext[POOL_HALO - j:POOL_HALO - j + n, lo:lo + POOL_GROUP] for j in range(w)]
        while len(terms) > 1:
            terms = [terms[a] + terms[a + 1] for a in range(0, len(terms), 2)]
        cnt = jnp.minimum(pos1, w).astype(F32)
        d = terms[0] / cnt - h[:, lo:lo + POOL_GROUP]
        mixed.append(_mm(d.astype(BF16), poolw_ref[lo:lo + POOL_GROUP, :]))
        yield
    y = jnp.concatenate(mixed, axis=-1) * gain(G_POOL_SCALE)
    x1 = x + _rms(y, gain(G_POST_MIX))
    yield
    return x1


def _emit_layer0(x3, gain, wkv_ref, xo_ref, kt_ref, v_ref):
    xo_ref[...] = x3
    hkv = _rms(x3, gain(G_KV)).astype(BF16)
    yield
    k = _mm(hkv, wkv_ref[:, :KV_WIDTH])
    for blk in range(x3.shape[0] // ATTN_BLOCK):
        kt_ref[blk] = k[blk * ATTN_BLOCK:(blk + 1) * ATTN_BLOCK, :].T.astype(BF16)
    yield
    v_ref[...] = _mm(hkv, wkv_ref[:, KV_WIDTH:]).astype(BF16)


def _layer0_body(x_ref, xhalo_ref, pa_ref, pb_ref, gains_ref, poolw_hbm, wgu_hbm, wdown_hbm, wgate_hbm,
                 wproj_hbm, wkv_hbm, xoa_ref, kta_ref, va_ref, xob_ref, ktb_ref, vb_ref,
                 hext_ref, acta_ref, actb_ref, x1b_ref, poolw_ref, wgu_ref, wdown_ref, wgate_ref, wproj_ref,
                 wkv_ref, stage_ref, sem, *, tiles_per_seq, n_tiles):
    step = pl.program_id(0)
    gains = gains_ref[...]
    gain = lambda r: gains[r:r + 1, :]
    seq_tile = jnp.minimum(step, n_tiles - 1) % tiles_per_seq
    resident = (poolw_ref, wgu_ref, wdown_ref, wgate_ref, wproj_ref, wkv_ref)

    def finish_b(poolw, wgu, wdown, wgate, wproj, wkv):
        x1 = x1b_ref[...]
        f = yield from _ffn_down(actb_ref, wdown)
        x3 = yield from _embed(x1, f, pb_ref[...], gain, wgate, wproj)
        yield from _emit_layer0(x3, gain, wkv, xob_ref, ktb_ref, vb_ref)

    def chain_a(poolw, wgu, wdown, wgate, wproj, wkv):
        x = x_ref[0:HALF_ROWS, :]
        h_hist = jnp.where(seq_tile == 0, 0.0, _rms(xhalo_ref[...], gain(G_PRE_MIX)))
        x1 = yield from _pool_mixer(x, h_hist, seq_tile * TILE_ROWS, gain, hext_ref.at[0], poolw)
        yield from _ffn_up(x1, gain, wgu, acta_ref)
        f = yield from _ffn_down(acta_ref, wdown)
        x3 = yield from _embed(x1, f, pa_ref[...], gain, wgate, wproj)
        yield from _emit_layer0(x3, gain, wkv, xoa_ref, kta_ref, va_ref)

    def start_b(poolw, wgu, wdown, wgate, wproj, wkv):
        x = x_ref[HALF_ROWS:, :]
        h_hist = _rms(x_ref[HALF_ROWS - POOL_HALO:HALF_ROWS, :], gain(G_PRE_MIX))
        x1 = yield from _pool_mixer(x, h_hist, seq_tile * TILE_ROWS + HALF_ROWS, gain, hext_ref.at[1], poolw)
        x1b_ref[...] = x1
        yield from _ffn_up(x1, gain, wgu, actb_ref)

    lead_b = L0_CHAIN_PIECES - L0_START_B_PIECES

    @pl.when(step == 0)
    def _():
        lazy = _LazyWeights((poolw_hbm.at[0], wgu_hbm.at[0], wdown_hbm.at[0], wgate_hbm.at[0], wproj_hbm.at[0],
                             wkv_hbm), resident, stage_ref, sem)
        _run_staggered([chain_a(*lazy.refs()), start_b(*lazy.refs())], lead=[lead_b])
        lazy.finish()

    @pl.when(jnp.logical_and(step > 0, step < n_tiles))
    def _():
        _run_staggered([finish_b(*resident), chain_a(*resident), start_b(*resident)], lead=[0, lead_b])

    @pl.when(step == n_tiles)
    def _():
        _run_staggered([finish_b(*resident)], lead=[])


def _attention_mixer(x, k_block, v_block, no_prev, gain, bias_ref, attn_ref, wq_ref, wo_ref):
    n = x.shape[0]
    hq = _rms(x, gain(G_PRE_MIX)).astype(BF16)
    yield
    q_pieces = []
    for lo, hi in _col_pieces(D_MODEL):
        q_pieces.append(_mm(hq, wq_ref[:, lo:hi]) * (HEAD_DIM ** -0.5))
        yield
    q = jnp.concatenate(q_pieces, axis=-1)

    left = lax.broadcasted_iota(jnp.int32, (ATTN_BLOCK, LANES), 1) < HEAD_DIM
    first_row = lax.broadcasted_iota(jnp.int32, (ATTN_BLOCK, LANES), 0) == 0
    first_col = lax.broadcasted_iota(jnp.int32, (LANES, ATTN_BLOCK), 1) == 0
    for j in range(n // ATTN_BLOCK):
        r0 = j * ATTN_BLOCK
        for kt in range(KV_TILES):
            kt_win = jnp.concatenate([jnp.where(first_col, 0, k_block(j - 1, kt)), k_block(j, kt)], axis=1)
            v_win = jnp.concatenate([jnp.where(first_row, 0, v_block(j - 1, kt)), v_block(j, kt)], axis=0)
            q_tiles = [q[r0:r0 + ATTN_BLOCK, (kt * GQA_GROUP + g) * LANES:(kt * GQA_GROUP + g + 1) * LANES]
                       for g in range(GQA_GROUP)]
            q8 = jnp.concatenate([jnp.where(left, t, 0.0) for t in q_tiles]
                                 + [jnp.where(left, 0.0, t) for t in q_tiles], axis=0).astype(BF16)
            s = _mm(q8, kt_win)
            s = s + bias_ref[kt]
            s_prev, s_cur = s[:, :ATTN_BLOCK], s[:, ATTN_BLOCK:]
            if j == 0 and no_prev is not None:
                real_key = lax.broadcasted_iota(jnp.int32, (1, LANES), 1) > 0
                s_prev = s_prev + jnp.where(jnp.logical_and(no_prev, real_key), NEG_INF, 0.0)
            m = jnp.max(jnp.maximum(s_prev, s_cur), axis=-1, keepdims=True)
            p_prev, p_cur = jnp.exp(s_prev - m), jnp.exp(s_cur - m)
            denom = jnp.sum(p_prev + p_cur, axis=-1, keepdims=True)
            probs = jnp.concatenate([p_prev, p_cur], axis=-1).astype(BF16)
            o = _mm(probs, v_win) / denom
            for g in range(GQA_GROUP):
                a = o[g * ATTN_BLOCK:(g + 1) * ATTN_BLOCK]
                b = o[(GQA_GROUP + g) * ATTN_BLOCK:(GQA_GROUP + g + 1) * ATTN_BLOCK]
                tile_lo = (kt * GQA_GROUP + g) * LANES
                attn_ref[r0:r0 + ATTN_BLOCK, tile_lo:tile_lo + LANES] = jnp.where(left, a, b).astype(BF16)
            yield

    y = []
    for lo, hi in _col_pieces(D_MODEL):
        y.append(_mm(attn_ref[...], wo_ref[:, lo:hi]))
        yield
    x1 = x + _rms(jnp.concatenate(y, axis=-1), gain(G_POST_MIX))
    yield
    return x1


def _layer1_body(xa_ref, xb_ref, kta_ref, ktb_ref, kthalo_ref, va_ref, vb_ref, vhalo_ref, p_ref, gains_ref, bias_ref,
                 wq_hbm, wo_hbm, wgu_hbm, wdown_hbm, wgate_hbm, wproj_hbm, xo_ref, attn_ref, act_ref,
                 wq_ref, wo_ref, wgu_ref, wdown_ref, wgate_ref, wproj_ref, stage_ref, sem, *, tiles_per_seq):
    step = pl.program_id(0)
    gains = gains_ref[...]
    gain = lambda r: gains[r:r + 1, :]
    blocks_per_half = HALF_ROWS // ATTN_BLOCK
    resident = (wq_ref, wo_ref, wgu_ref, wdown_ref, wgate_ref, wproj_ref)

    def k_block(blk, kt):
        if blk < 0:
            ref, i = kthalo_ref, 0
        else:
            ref, i = (kta_ref if blk < blocks_per_half else ktb_ref), blk % blocks_per_half
        return ref[i, kt * LANES:(kt + 1) * LANES, :]

    def v_block(blk, kt):
        if blk < 0:
            return vhalo_ref[:, kt * LANES:(kt + 1) * LANES]
        ref = va_ref if blk < blocks_per_half else vb_ref
        r = (blk % blocks_per_half) * ATTN_BLOCK
        return ref[r:r + ATTN_BLOCK, kt * LANES:(kt + 1) * LANES]

    def chain(wq, wo, wgu, wdown, wgate, wproj):
        x = jnp.concatenate([xa_ref[...], xb_ref[...]], axis=0)
        x1 = yield from _attention_mixer(x, k_block, v_block, step % tiles_per_seq == 0, gain, bias_ref, attn_ref,
                                         wq, wo)
        yield from _ffn_up(x1, gain, wgu, act_ref)
        f = yield from _ffn_down(act_ref, wdown)
        xo_ref[...] = yield from _embed(x1, f, p_ref[...], gain, wgate, wproj)

    @pl.when(step == 0)
    def _():
        lazy = _LazyWeights((wq_hbm, wo_hbm, wgu_hbm.at[1], wdown_hbm.at[1], wgate_hbm.at[1], wproj_hbm.at[1]),
                            resident, stage_ref, sem)
        _run_staggered([chain(*lazy.refs())], lead=[])
        lazy.finish()

    @pl.when(step > 0)
    def _():
        _run_staggered([chain(*resident)], lead=[])


def _head_order():
    order = []
    for kt in range(KV_TILES):
        for g in range(GQA_GROUP):
            order += [(2 * kt) * GQA_GROUP + g, (2 * kt + 1) * GQA_GROUP + g]
    return order


def _permute_heads(w, axis):
    return jnp.concatenate([lax.slice_in_dim(w, h * HEAD_DIM, (h + 1) * HEAD_DIM, axis=axis)
                            for h in _head_order()], axis=axis)


def _attention_bias(sinks):
    qi = jnp.arange(ATTN_BLOCK)[:, None]
    si = jnp.arange(2 * ATTN_BLOCK)[None, :]
    rel = ATTN_BLOCK + qi - si
    valid = (rel >= 0) & (rel < ATTN_BLOCK)
    heads = jnp.arange(1, N_HEADS + 1, dtype=F32)
    slopes = jnp.exp2(-8.0 * heads / N_HEADS).reshape(N_KV_HEADS, GQA_GROUP)
    bias = jnp.where(valid[None, None], -slopes[:, :, None, None] * rel.astype(F32)[None, None], NEG_INF)
    sink = sinks.astype(F32).reshape(N_KV_HEADS, GQA_GROUP, 1, 1)
    bias = jnp.where(si[None, None] == 0, sink, bias)
    return bias.reshape(KV_TILES, HEADS_PER_KV_TILE * ATTN_BLOCK, 2 * ATTN_BLOCK)


def kernel(x, p, pre_mix_g, post_mix_g, pre_ffn_g, post_ffn_g, pool_w, pool_scale, kv_g, w_kv, w_q, sinks,
           w_o, w_gu, w_down, ple_g, w_ple_gate, w_ple_proj, ple_post_g):
    batch, seq, _ = x.shape
    tm, hm = TILE_ROWS, HALF_ROWS
    assert seq % tm == 0 and hm % ATTN_BLOCK == 0 and hm % POOL_HALO == 0
    tokens = batch * seq
    n_tiles = tokens // tm
    tiles_per_seq = seq // tm

    def gains(i, extra_a, extra_b):
        return jnp.stack([pre_mix_g[i], post_mix_g[i], pre_ffn_g[i], post_ffn_g[i], ple_g[i],
                          ple_post_g[i], extra_a, extra_b]).astype(F32)

    zeros = jnp.zeros((D_MODEL,), F32)
    gains0 = gains(0, pool_scale[0], kv_g)
    gains1 = gains(1, zeros, zeros)
    bias = _attention_bias(sinks[0])
    pool_w_rows = pool_w.reshape(pool_w.shape[0], D_MODEL, POOL_GROUP)
    x_rows = x.reshape(tokens, D_MODEL)
    p_rows = p.reshape(p.shape[0], tokens, PLE_DIM)

    in_hbm = pl.BlockSpec(memory_space=pl.ANY)
    bf16_weight = lambda rows, cols: pltpu.VMEM((rows, cols), BF16)
    ffn_weights = [bf16_weight(D_MODEL, 2 * D_FF), bf16_weight(D_FF, D_MODEL), bf16_weight(D_MODEL, D_MODEL),
                   bf16_weight(PLE_DIM, D_MODEL)]
    staging = [pltpu.VMEM((2, STAGE_ROWS, STAGE_COLS), F32), pltpu.SemaphoreType.DMA((2,))]

    this_tile = lambda s: jnp.minimum(s, n_tiles - 1)
    prev_tile = lambda s: jnp.maximum(s - 1, 0)
    whole = lambda shape: pl.BlockSpec(shape, lambda s: (0,) * len(shape), pipeline_mode=pl.Buffered(1))
    half_a = lambda width: pl.BlockSpec((hm, width), lambda s: (this_tile(s), 0))
    half_b = lambda width: pl.BlockSpec((hm, width), lambda s: (prev_tile(s), 0))
    half_out = lambda width, dtype: jax.ShapeDtypeStruct((tokens // 2, width), dtype)
    blocks_per_half = hm // ATTN_BLOCK
    kt_blocks = lambda n, block: pl.BlockSpec((n, KV_WIDTH, ATTN_BLOCK), lambda s: (block(s), 0, 0))
    kt_out = jax.ShapeDtypeStruct((n_tiles * blocks_per_half, KV_WIDTH, ATTN_BLOCK), BF16)
    x3a, kta, va, x3b, ktb, vb = pl.pallas_call(
        functools.partial(_layer0_body, tiles_per_seq=tiles_per_seq, n_tiles=n_tiles),
        grid=(n_tiles + 1,),
        in_specs=[pl.BlockSpec((tm, D_MODEL), lambda s: (this_tile(s), 0)),
                  pl.BlockSpec((POOL_HALO, D_MODEL),
                               lambda s: (jnp.maximum(this_tile(s) * (tm // POOL_HALO) - 1, 0), 0)),
                  pl.BlockSpec((None, hm, PLE_DIM), lambda s: (0, 2 * this_tile(s), 0)),
                  pl.BlockSpec((None, hm, PLE_DIM), lambda s: (0, 2 * prev_tile(s) + 1, 0)),
                  whole((8, D_MODEL))] + [in_hbm] * 6,
        out_specs=[half_a(D_MODEL), kt_blocks(blocks_per_half, this_tile), half_a(KV_WIDTH),
                   half_b(D_MODEL), kt_blocks(blocks_per_half, prev_tile), half_b(KV_WIDTH)],
        out_shape=[half_out(D_MODEL, F32), kt_out, half_out(KV_WIDTH, BF16)] * 2,
        scratch_shapes=[pltpu.VMEM((2, hm + POOL_HALO, D_MODEL), F32), pltpu.VMEM((hm, D_FF), BF16),
                        pltpu.VMEM((hm, D_FF), BF16), pltpu.VMEM((hm, D_MODEL), F32),
                        bf16_weight(D_MODEL, POOL_GROUP)] + ffn_weights
                       + [bf16_weight(D_MODEL, 2 * KV_WIDTH)] + staging,
        compiler_params=pltpu.CompilerParams(dimension_semantics=("arbitrary",),
                                             vmem_limit_bytes=VMEM_LIMIT_BYTES),
        name="yoco_layer0",
    )(x_rows, x_rows, p_rows, p_rows, gains0, pool_w_rows, w_gu, w_down, w_ple_gate, w_ple_proj, w_kv)

    rows_of = lambda rows, width, block: pl.BlockSpec((rows, width), lambda g: (block(g), 0))
    tile_half = lambda width: rows_of(hm, width, lambda g: g)
    last_block_before = lambda g: jnp.maximum(g * blocks_per_half - 1, 0)
    kv_before = rows_of(ATTN_BLOCK, KV_WIDTH, last_block_before)
    kt_tile = kt_blocks(blocks_per_half, lambda g: g)
    out = pl.pallas_call(
        functools.partial(_layer1_body, tiles_per_seq=tiles_per_seq),
        grid=(n_tiles,),
        in_specs=[tile_half(D_MODEL), tile_half(D_MODEL), kt_tile, kt_tile, kt_blocks(1, last_block_before),
                  tile_half(KV_WIDTH), tile_half(KV_WIDTH), kv_before,
                  pl.BlockSpec((None, tm, PLE_DIM), lambda g: (1, g, 0)), whole((8, D_MODEL)),
                  whole((KV_TILES, HEADS_PER_KV_TILE * ATTN_BLOCK, 2 * ATTN_BLOCK))] + [in_hbm] * 6,
        out_specs=rows_of(tm, D_MODEL, lambda g: g),
        out_shape=jax.ShapeDtypeStruct((tokens, D_MODEL), F32),
        scratch_shapes=[pltpu.VMEM((tm, D_MODEL), BF16), pltpu.VMEM((tm, D_FF), BF16),
                        bf16_weight(D_MODEL, D_MODEL), bf16_weight(D_MODEL, D_MODEL)] + ffn_weights + staging,
        compiler_params=pltpu.CompilerParams(dimension_semantics=("arbitrary",),
                                             vmem_limit_bytes=VMEM_LIMIT_BYTES),
        name="yoco_layer1",
    )(x3a, x3b, kta, ktb, ktb, va, vb, vb, p_rows, gains1, bias, _permute_heads(w_q[0], 1), _permute_heads(w_o[0], 0),
      w_gu, w_down, w_ple_gate, w_ple_proj)
    return out.reshape(batch, seq, D_MODEL)
```

```python
import functools

import jax
import jax.numpy as jnp
from jax import lax
from jax.experimental import pallas as pl
from jax.experimental.pallas import tpu as pltpu

D_MODEL = 1024
POOL_WINDOWS = (2, 4, 8, 16)
POOL_GROUP = D_MODEL // len(POOL_WINDOWS)
POOL_HALO = 16
HEAD_DIM = 64
N_HEADS = D_MODEL // HEAD_DIM
N_KV_HEADS = 4
GQA_GROUP = N_HEADS // N_KV_HEADS
ATTN_BLOCK = 128
D_FF = 2816
PLE_DIM = 256
EPS = 1e-6
NEG_INF = -1e30
LANES = 128
KV_WIDTH = N_KV_HEADS * HEAD_DIM
KV_TILES = KV_WIDTH // LANES
HEADS_PER_KV_TILE = 2 * GQA_GROUP

TILE_ROWS = 512
HALF_ROWS = TILE_ROWS // 2
FF_CHUNK = 256
COL_PIECE = 256
STAGE_ROWS, STAGE_COLS = 1024, 512
VMEM_LIMIT_BYTES = 56 * 1024 * 1024

BF16 = jnp.bfloat16
F32 = jnp.float32

G_PRE_MIX, G_POST_MIX, G_PRE_FFN, G_POST_FFN, G_PLE, G_PLE_POST, G_POOL_SCALE, G_KV = range(8)


def _rms(x, g):
    return x * lax.rsqrt(jnp.mean(x * x, axis=-1, keepdims=True) + EPS) * g


def _sigmoid(x):
    return 1.0 / (1.0 + jnp.exp(-x))


def _mm(a, b):
    return jnp.dot(a, b, preferred_element_type=F32)


def _gate_up_first_use(c0, nc):
    return min(c for c in range(D_FF // FF_CHUNK) for lo in (c * FF_CHUNK, D_FF + c * FF_CHUNK)
               if lo < c0 + nc and lo + FF_CHUNK > c0)


class _LazyWeights:
    def __init__(self, hbm_refs, vmem_refs, stage_ref, sem):
        self.vmem_refs, self.stage_ref, self.sem = vmem_refs, stage_ref, sem
        chunks = []
        for w, (src, dst) in enumerate(zip(hbm_refs, vmem_refs)):
            rows, cols = dst.shape
            for c0 in range(0, cols, STAGE_COLS):
                nc = min(STAGE_COLS, cols - c0)
                use = _gate_up_first_use(c0, nc) if cols == 2 * D_FF else c0
                for r0 in range(0, rows, STAGE_ROWS):
                    chunks.append(((w, use, c0, r0), src, dst, r0, min(STAGE_ROWS, rows - r0), c0, nc))
        self.chunks = sorted(chunks, key=lambda c: c[0])
        self.done = 0
        self._copy(0).start()

    def _copy(self, i):
        _, src, _, r0, nr, c0, nc = self.chunks[i]
        return pltpu.make_async_copy(src.at[pl.ds(r0, nr), pl.ds(c0, nc)],
                                     self.stage_ref.at[i % 2, pl.ds(0, nr), pl.ds(0, nc)], self.sem.at[i % 2])

    def _convert_next(self):
        i = self.done
        if i + 1 < len(self.chunks):
            self._copy(i + 1).start()
        self._copy(i).wait()
        _, _, dst, r0, nr, c0, nc = self.chunks[i]
        dst[r0:r0 + nr, c0:c0 + nc] = self.stage_ref[i % 2, 0:nr, 0:nc].astype(BF16)
        self.done += 1

    def need(self, w, rows, cols):
        last = max(i for i, (key, _, _, r0, nr, c0, nc) in enumerate(self.chunks)
                   if key[0] == w and r0 < rows[1] and r0 + nr > rows[0] and c0 < cols[1] and c0 + nc > cols[0])
        while self.done <= last:
            self._convert_next()

    def finish(self):
        while self.done < len(self.chunks):
            self._convert_next()

    def refs(self):
        return tuple(_LazyRef(self, w) for w in range(len(self.vmem_refs)))


class _LazyRef:
    def __init__(self, weights, w):
        self.weights, self.w = weights, w

    def __getitem__(self, idx):
        ref = self.weights.vmem_refs[self.w]
        (r_lo, r_hi, _), (c_lo, c_hi, _) = (sl.indices(size) for sl, size in zip(idx, ref.shape))
        self.weights.need(self.w, (r_lo, r_hi), (c_lo, c_hi))
        return ref[idx]


def _run_staggered(chains, lead):
    chains = list(chains)
    count, done = [0] * len(chains), [False] * len(chains)
    while not all(done):
        for i, chain in enumerate(chains):
            if i > 0 and count[i - 1] < lead[i - 1] and not done[i - 1]:
                break
            if done[i]:
                continue
            try:
                next(chain)
                count[i] += 1
            except StopIteration:
                done[i] = True


def _col_pieces(width):
    return [(lo, lo + COL_PIECE) for lo in range(0, width, COL_PIECE)]


def _ffn_up(x1, gain, wgu_ref, act_ref):
    h = _rms(x1, gain(G_PRE_FFN)).astype(BF16)
    for lo in range(0, D_FF, FF_CHUNK):
        gate = _mm(h, wgu_ref[:, lo:lo + FF_CHUNK])
        up = _mm(h, wgu_ref[:, D_FF + lo:D_FF + lo + FF_CHUNK])
        act_ref[:, lo:lo + FF_CHUNK] = (gate * _sigmoid(gate) * up).astype(BF16)
        yield


def _ffn_down(act_ref, wdown_ref):
    f = []
    for lo, hi in _col_pieces(D_MODEL):
        f.append(_mm(act_ref[...], wdown_ref[:, lo:hi]))
        yield
    return jnp.concatenate(f, axis=-1)


def _embed(x1, f, p, gain, wgate_ref, wproj_ref):
    x2 = x1 + _rms(f, gain(G_POST_FFN))
    hg = _rms(x2, gain(G_PLE)).astype(BF16)
    pb = p.astype(BF16)
    yield
    e = []
    for lo, hi in _col_pieces(D_MODEL):
        e.append(_mm(pb, wproj_ref[:, lo:hi]) * _sigmoid(_mm(hg, wgate_ref[:, lo:hi])))
        yield
    return x2 + _rms(jnp.concatenate(e, axis=-1), gain(G_PLE_POST))


L0_MIX_PIECES = 2 + len(POOL_WINDOWS)
L0_START_B_PIECES = L0_MIX_PIECES + D_FF // FF_CHUNK
L0_CHAIN_PIECES = L0_START_B_PIECES + 2 * (D_MODEL // COL_PIECE) + 1 + 3


def _pool_mixer(x, h_hist, pos0, gain, hext, poolw_ref):
    n = x.shape[0]
    h = _rms(x, gain(G_PRE_MIX))
    hext[0:POOL_HALO, :] = h_hist
    hext[POOL_HALO:POOL_HALO + n, :] = h
    yield
    pos1 = lax.broadcasted_iota(jnp.int32, (n, POOL_GROUP), 0) + (pos0 + 1)
    mixed = []
    for gi, w in enumerate(POOL_WINDOWS):
        lo = gi * POOL_GROUP
        terms = [hext[POOL_HALO - j:POOL_HALO - j + n, lo:lo + POOL_GROUP] for j in range(w)]
        while len(terms) > 1:
            terms = [terms[a] + terms[a + 1] for a in range(0, len(terms), 2)]
        cnt = jnp.minimum(pos1, w).astype(F32)
        d = terms[0] / cnt - h[:, lo:lo + POOL_GROUP]
        mixed.append(_mm(d.astype(BF16), poolw_ref[lo:lo + POOL_GROUP, :]))
        yield
    y = jnp.concatenate(mixed, axis=-1) * gain(G_POOL_SCALE)
    x1 = x + _rms(y, gain(G_POST_MIX))
    yield
    return x1


def _emit_layer0(x3, gain, wkv_ref, xo_ref, kt_ref, v_ref):
    xo_ref[...] = x3
    hkv = _rms(x3, gain(G_KV)).astype(BF16)
    yield
    k = _mm(hkv, wkv_ref[:, :KV_WIDTH])
    for blk in range(x3.shape[0] // ATTN_BLOCK):
        kt_ref[blk] = k[blk * ATTN_BLOCK:(blk + 1) * ATTN_BLOCK, :].T.astype(BF16)
    yield
    v_ref[...] = _mm(hkv, wkv_ref[:, KV_WIDTH:]).astype(BF16)


def _layer0_body(x_ref, xhalo_ref, pa_ref, pb_ref, gains_ref, poolw_hbm, wgu_hbm, wdown_hbm, wgate_hbm,
                 wproj_hbm, wkv_hbm, xoa_ref, kta_ref, va_ref, xob_ref, ktb_ref, vb_ref,
                 hext_ref, acta_ref, actb_ref, x1b_ref, poolw_ref, wgu_ref, wdown_ref, wgate_ref, wproj_ref,
                 wkv_ref, stage_ref, sem, *, tiles_per_seq, n_tiles):
    step = pl.program_id(0)
    gains = gains_ref[...]
    gain = lambda r: gains[r:r + 1, :]
    seq_tile = jnp.minimum(step, n_tiles - 1) % tiles_per_seq
    resident = (poolw_ref, wgu_ref, wdown_ref, wgate_ref, wproj_ref, wkv_ref)

    def finish_b(poolw, wgu, wdown, wgate, wproj, wkv):
        x1 = x1b_ref[...]
        f = yield from _ffn_down(actb_ref, wdown)
        x3 = yield from _embed(x1, f, pb_ref[...], gain, wgate, wproj)
        yield from _emit_layer0(x3, gain, wkv, xob_ref, ktb_ref, vb_ref)

    def chain_a(poolw, wgu, wdown, wgate, wproj, wkv):
        x = x_ref[0:HALF_ROWS, :]
        h_hist = jnp.where(seq_tile == 0, 0.0, _rms(xhalo_ref[...], gain(G_PRE_MIX)))
        x1 = yield from _pool_mixer(x, h_hist, seq_tile * TILE_ROWS, gain, hext_ref.at[0], poolw)
        yield from _ffn_up(x1, gain, wgu, acta_ref)
        f = yield from _ffn_down(acta_ref, wdown)
        x3 = yield from _embed(x1, f, pa_ref[...], gain, wgate, wproj)
        yield from _emit_layer0(x3, gain, wkv, xoa_ref, kta_ref, va_ref)

    def start_b(poolw, wgu, wdown, wgate, wproj, wkv):
        x = x_ref[HALF_ROWS:, :]
        h_hist = _rms(x_ref[HALF_ROWS - POOL_HALO:HALF_ROWS, :], gain(G_PRE_MIX))
        x1 = yield from _pool_mixer(x, h_hist, seq_tile * TILE_ROWS + HALF_ROWS, gain, hext_ref.at[1], poolw)
        x1b_ref[...] = x1
        yield from _ffn_up(x1, gain, wgu, actb_ref)

    lead_b = L0_CHAIN_PIECES - L0_START_B_PIECES

    @pl.when(step == 0)
    def _():
        lazy = _LazyWeights((poolw_hbm.at[0], wgu_hbm.at[0], wdown_hbm.at[0], wgate_hbm.at[0], wproj_hbm.at[0],
                             wkv_hbm), resident, stage_ref, sem)
        _run_staggered([chain_a(*lazy.refs()), start_b(*lazy.refs())], lead=[lead_b])
        lazy.finish()

    @pl.when(jnp.logical_and(step > 0, step < n_tiles))
    def _():
        _run_staggered([finish_b(*resident), chain_a(*resident), start_b(*resident)], lead=[0, lead_b])

    @pl.when(step == n_tiles)
    def _():
        _run_staggered([finish_b(*resident)], lead=[])


def _attention_mixer(x, k_block, v_block, no_prev, gain, bias_ref, attn_ref, wq_ref, wo_ref):
    n = x.shape[0]
    hq = _rms(x, gain(G_PRE_MIX)).astype(BF16)
    yield
    q_pieces = []
    for lo, hi in _col_pieces(D_MODEL):
        q_pieces.append(_mm(hq, wq_ref[:, lo:hi]) * (HEAD_DIM ** -0.5))
        yield
    q = jnp.concatenate(q_pieces, axis=-1)

    left = lax.broadcasted_iota(jnp.int32, (ATTN_BLOCK, LANES), 1) < HEAD_DIM
    first_row = lax.broadcasted_iota(jnp.int32, (ATTN_BLOCK, LANES), 0) == 0
    first_col = lax.broadcasted_iota(jnp.int32, (LANES, ATTN_BLOCK), 1) == 0
    for j in range(n // ATTN_BLOCK):
        r0 = j * ATTN_BLOCK
        for kt in range(KV_TILES):
            kt_win = jnp.concatenate([jnp.where(first_col, 0, k_block(j - 1, kt)), k_block(j, kt)], axis=1)
            v_win = jnp.concatenate([jnp.where(first_row, 0, v_block(j - 1, kt)), v_block(j, kt)], axis=0)
            q_tiles = [q[r0:r0 + ATTN_BLOCK, (kt * GQA_GROUP + g) * LANES:(kt * GQA_GROUP + g + 1) * LANES]
                       for g in range(GQA_GROUP)]
            q8 = jnp.concatenate([jnp.where(left, t, 0.0) for t in q_tiles]
                                 + [jnp.where(left, 0.0, t) for t in q_tiles], axis=0).astype(BF16)
            s = _mm(q8, kt_win)
            s = s + bias_ref[kt]
            s_prev, s_cur = s[:, :ATTN_BLOCK], s[:, ATTN_BLOCK:]
            if j == 0 and no_prev is not None:
                real_key = lax.broadcasted_iota(jnp.int32, (1, LANES), 1) > 0
                s_prev = s_prev + jnp.where(jnp.logical_and(no_prev, real_key), NEG_INF, 0.0)
            m = jnp.max(jnp.maximum(s_prev, s_cur), axis=-1, keepdims=True)
            p_prev, p_cur = jnp.exp(s_prev - m), jnp.exp(s_cur - m)
            denom = jnp.sum(p_prev + p_cur, axis=-1, keepdims=True)
            probs = jnp.concatenate([p_prev, p_cur], axis=-1).astype(BF16)
            o = _mm(probs, v_win) / denom
            for g in range(GQA_GROUP):
                a = o[g * ATTN_BLOCK:(g + 1) * ATTN_BLOCK]
                b = o[(GQA_GROUP + g) * ATTN_BLOCK:(GQA_GROUP + g + 1) * ATTN_BLOCK]
                tile_lo = (kt * GQA_GROUP + g) * LANES
                attn_ref[r0:r0 + ATTN_BLOCK, tile_lo:tile_lo + LANES] = jnp.where(left, a, b).astype(BF16)
            yield

    y = []
    for lo, hi in _col_pieces(D_MODEL):
        y.append(_mm(attn_ref[...], wo_ref[:, lo:hi]))
        yield
    x1 = x + _rms(jnp.concatenate(y, axis=-1), gain(G_POST_MIX))
    yield
    return x1


def _layer1_body(xa_ref, xb_ref, kta_ref, ktb_ref, kthalo_ref, va_ref, vb_ref, vhalo_ref, p_ref, gains_ref, bias_ref,
                 wq_hbm, wo_hbm, wgu_hbm, wdown_hbm, wgate_hbm, wproj_hbm, xo_ref, attn_ref, act_ref,
                 wq_ref, wo_ref, wgu_ref, wdown_ref, wgate_ref, wproj_ref, stage_ref, sem, *, tiles_per_seq):
    step = pl.program_id(0)
    gains = gains_ref[...]
    gain = lambda r: gains[r:r + 1, :]
    blocks_per_half = HALF_ROWS // ATTN_BLOCK
    resident = (wq_ref, wo_ref, wgu_ref, wdown_ref, wgate_ref, wproj_ref)

    def k_block(blk, kt):
        if blk < 0:
            ref, i = kthalo_ref, 0
        else:
            ref, i = (kta_ref if blk < blocks_per_half else ktb_ref), blk % blocks_per_half
        return ref[i, kt * LANES:(kt + 1) * LANES, :]

    def v_block(blk, kt):
        if blk < 0:
            return vhalo_ref[:, kt * LANES:(kt + 1) * LANES]
        ref = va_ref if blk < blocks_per_half else vb_ref
        r = (blk % blocks_per_half) * ATTN_BLOCK
        return ref[r:r + ATTN_BLOCK, kt * LANES:(kt + 1) * LANES]

    def chain(wq, wo, wgu, wdown, wgate, wproj):
        x = jnp.concatenate([xa_ref[...], xb_ref[...]], axis=0)
        x1 = yield from _attention_mixer(x, k_block, v_block, step % tiles_per_seq == 0, gain, bias_ref, attn_ref,
                                         wq, wo)
        yield from _ffn_up(x1, gain, wgu, act_ref)
        f = yield from _ffn_down(act_ref, wdown)
        xo_ref[...] = yield from _embed(x1, f, p_ref[...], gain, wgate, wproj)

    @pl.when(step == 0)
    def _():
        _LazyWeights((wq_hbm, wo_hbm, wgu_hbm.at[1], wdown_hbm.at[1], wgate_hbm.at[1], wproj_hbm.at[1]),
                     resident, stage_ref, sem).finish()

    _run_staggered([chain(*resident)], lead=[])


def _head_order():
    order = []
    for kt in range(KV_TILES):
        for g in range(GQA_GROUP):
            order += [(2 * kt) * GQA_GROUP + g, (2 * kt + 1) * GQA_GROUP + g]
    return order


def _permute_heads(w, axis):
    return jnp.concatenate([lax.slice_in_dim(w, h * HEAD_DIM, (h + 1) * HEAD_DIM, axis=axis)
                            for h in _head_order()], axis=axis)


def _attention_bias(sinks):
    qi = jnp.arange(ATTN_BLOCK)[:, None]
    si = jnp.arange(2 * ATTN_BLOCK)[None, :]
    rel = ATTN_BLOCK + qi - si
    valid = (rel >= 0) & (rel < ATTN_BLOCK)
    heads = jnp.arange(1, N_HEADS + 1, dtype=F32)
    slopes = jnp.exp2(-8.0 * heads / N_HEADS).reshape(N_KV_HEADS, GQA_GROUP)
    bias = jnp.where(valid[None, None], -slopes[:, :, None, None] * rel.astype(F32)[None, None], NEG_INF)
    sink = sinks.astype(F32).reshape(N_KV_HEADS, GQA_GROUP, 1, 1)
    bias = jnp.where(si[None, None] == 0, sink, bias)
    return bias.reshape(KV_TILES, HEADS_PER_KV_TILE * ATTN_BLOCK, 2 * ATTN_BLOCK)


def kernel(x, p, pre_mix_g, post_mix_g, pre_ffn_g, post_ffn_g, pool_w, pool_scale, kv_g, w_kv, w_q, sinks,
           w_o, w_gu, w_down, ple_g, w_ple_gate, w_ple_proj, ple_post_g):
    batch, seq, _ = x.shape
    tm, hm = TILE_ROWS, HALF_ROWS
    assert seq % tm == 0 and hm % ATTN_BLOCK == 0 and hm % POOL_HALO == 0
    tokens = batch * seq
    n_tiles = tokens // tm
    tiles_per_seq = seq // tm

    def gains(i, extra_a, extra_b):
        return jnp.stack([pre_mix_g[i], post_mix_g[i], pre_ffn_g[i], post_ffn_g[i], ple_g[i],
                          ple_post_g[i], extra_a, extra_b]).astype(F32)

    zeros = jnp.zeros((D_MODEL,), F32)
    gains0 = gains(0, pool_scale[0], kv_g)
    gains1 = gains(1, zeros, zeros)
    bias = _attention_bias(sinks[0])
    pool_w_rows = pool_w.reshape(pool_w.shape[0], D_MODEL, POOL_GROUP)
    x_rows = x.reshape(tokens, D_MODEL)
    p_rows = p.reshape(p.shape[0], tokens, PLE_DIM)

    in_hbm = pl.BlockSpec(memory_space=pl.ANY)
    bf16_weight = lambda rows, cols: pltpu.VMEM((rows, cols), BF16)
    ffn_weights = [bf16_weight(D_MODEL, 2 * D_FF), bf16_weight(D_FF, D_MODEL), bf16_weight(D_MODEL, D_MODEL),
                   bf16_weight(PLE_DIM, D_MODEL)]
    staging = [pltpu.VMEM((2, STAGE_ROWS, STAGE_COLS), F32), pltpu.SemaphoreType.DMA((2,))]

    this_tile = lambda s: jnp.minimum(s, n_tiles - 1)
    prev_tile = lambda s: jnp.maximum(s - 1, 0)
    whole = lambda shape: pl.BlockSpec(shape, lambda s: (0,) * len(shape), pipeline_mode=pl.Buffered(1))
    half_a = lambda width: pl.BlockSpec((hm, width), lambda s: (this_tile(s), 0))
    half_b = lambda width: pl.BlockSpec((hm, width), lambda s: (prev_tile(s), 0))
    half_out = lambda width, dtype: jax.ShapeDtypeStruct((tokens // 2, width), dtype)
    blocks_per_half = hm // ATTN_BLOCK
    kt_blocks = lambda n, block: pl.BlockSpec((n, KV_WIDTH, ATTN_BLOCK), lambda s: (block(s), 0, 0))
    kt_out = jax.ShapeDtypeStruct((n_tiles * blocks_per_half, KV_WIDTH, ATTN_BLOCK), BF16)
    x3a, kta, va, x3b, ktb, vb = pl.pallas_call(
        functools.partial(_layer0_body, tiles_per_seq=tiles_per_seq, n_tiles=n_tiles),
        grid=(n_tiles + 1,),
        in_specs=[pl.BlockSpec((tm, D_MODEL), lambda s: (this_tile(s), 0)),
                  pl.BlockSpec((POOL_HALO, D_MODEL),
                               lambda s: (jnp.maximum(this_tile(s) * (tm // POOL_HALO) - 1, 0), 0)),
                  pl.BlockSpec((None, hm, PLE_DIM), lambda s: (0, 2 * this_tile(s), 0)),
                  pl.BlockSpec((None, hm, PLE_DIM), lambda s: (0, 2 * prev_tile(s) + 1, 0)),
                  whole((8, D_MODEL))] + [in_hbm] * 6,
        out_specs=[half_a(D_MODEL), kt_blocks(blocks_per_half, this_tile), half_a(KV_WIDTH),
                   half_b(D_MODEL), kt_blocks(blocks_per_half, prev_tile), half_b(KV_WIDTH)],
        out_shape=[half_out(D_MODEL, F32), kt_out, half_out(KV_WIDTH, BF16)] * 2,
        scratch_shapes=[pltpu.VMEM((2, hm + POOL_HALO, D_MODEL), F32), pltpu.VMEM((hm, D_FF), BF16),
                        pltpu.VMEM((hm, D_FF), BF16), pltpu.VMEM((hm, D_MODEL), F32),
                        bf16_weight(D_MODEL, POOL_GROUP)] + ffn_weights
                       + [bf16_weight(D_MODEL, 2 * KV_WIDTH)] + staging,
        compiler_params=pltpu.CompilerParams(dimension_semantics=("arbitrary",),
                                             vmem_limit_bytes=VMEM_LIMIT_BYTES),
        name="yoco_layer0",
    )(x_rows, x_rows, p_rows, p_rows, gains0, pool_w_rows, w_gu, w_down, w_ple_gate, w_ple_proj, w_kv)

    rows_of = lambda rows, width, block: pl.BlockSpec((rows, width), lambda g: (block(g), 0))
    tile_half = lambda width: rows_of(hm, width, lambda g: g)
    last_block_before = lambda g: jnp.maximum(g * blocks_per_half - 1, 0)
    kv_before = rows_of(ATTN_BLOCK, KV_WIDTH, last_block_before)
    kt_tile = kt_blocks(blocks_per_half, lambda g: g)
    out = pl.pallas_call(
        functools.partial(_layer1_body, tiles_per_seq=tiles_per_seq),
        grid=(n_tiles,),
        in_specs=[tile_half(D_MODEL), tile_half(D_MODEL), kt_tile, kt_tile, kt_blocks(1, last_block_before),
                  tile_half(KV_WIDTH), tile_half(KV_WIDTH), kv_before,
                  pl.BlockSpec((None, tm, PLE_DIM), lambda g: (1, g, 0)), whole((8, D_MODEL)),
                  whole((KV_TILES, HEADS_PER_KV_TILE * ATTN_BLOCK, 2 * ATTN_BLOCK))] + [in_hbm] * 6,
        out_specs=rows_of(tm, D_MODEL, lambda g: g),
        out_shape=jax.ShapeDtypeStruct((tokens, D_MODEL), F32),
        scratch_shapes=[pltpu.VMEM((tm, D_MODEL), BF16), pltpu.VMEM((tm, D_FF), BF16),
                        bf16_weight(D_MODEL, D_MODEL), bf16_weight(D_MODEL, D_MODEL)] + ffn_weights + staging,
        compiler_params=pltpu.CompilerParams(dimension_semantics=("arbitrary",),
                                             vmem_limit_bytes=VMEM_LIMIT_BYTES),
        name="yoco_layer1",
    )(x3a, x3b, kta, ktb, ktb, va, vb, vb, p_rows, gains1, bias, _permute_heads(w_q[0], 1), _permute_heads(w_o[0], 0),
      w_gu, w_down, w_ple_gate, w_ple_proj)
    return out.reshape(batch, seq, D_MODEL)
```

```python
import functools

import jax
import jax.numpy as jnp
from jax import lax
from jax.experimental import pallas as pl
from jax.experimental.pallas import tpu as pltpu

D_MODEL = 1024
POOL_WINDOWS = (2, 4, 8, 16)
POOL_GROUP = D_MODEL // len(POOL_WINDOWS)
POOL_HALO = 16
HEAD_DIM = 64
N_HEADS = D_MODEL // HEAD_DIM
N_KV_HEADS = 4
GQA_GROUP = N_HEADS // N_KV_HEADS
ATTN_BLOCK = 128
D_FF = 2816
PLE_DIM = 256
EPS = 1e-6
NEG_INF = -1e30
LANES = 128
KV_WIDTH = N_KV_HEADS * HEAD_DIM
KV_TILES = KV_WIDTH // LANES
HEADS_PER_KV_TILE = 2 * GQA_GROUP

TILE_ROWS = 512
HALF_ROWS = TILE_ROWS // 2
FF_CHUNK = 256
COL_PIECE = 256
STAGE_ROWS, STAGE_COLS = 1024, 512
VMEM_LIMIT_BYTES = 56 * 1024 * 1024

BF16 = jnp.bfloat16
F32 = jnp.float32

G_PRE_MIX, G_POST_MIX, G_PRE_FFN, G_POST_FFN, G_PLE, G_PLE_POST, G_POOL_SCALE, G_KV = range(8)


def _rms(x, g):
    return x * lax.rsqrt(jnp.mean(x * x, axis=-1, keepdims=True) + EPS) * g


def _sigmoid(x):
    return 1.0 / (1.0 + jnp.exp(-x))


def _mm(a, b):
    return jnp.dot(a, b, preferred_element_type=F32)


def _gate_up_first_use(c0, nc):
    return min(c for c in range(D_FF // FF_CHUNK) for lo in (c * FF_CHUNK, D_FF + c * FF_CHUNK)
               if lo < c0 + nc and lo + FF_CHUNK > c0)


class _LazyWeights:
    def __init__(self, hbm_refs, vmem_refs, stage_ref, sem):
        self.vmem_refs, self.stage_ref, self.sem = vmem_refs, stage_ref, sem
        chunks = []
        for w, (src, dst) in enumerate(zip(hbm_refs, vmem_refs)):
            rows, cols = dst.shape
            for c0 in range(0, cols, STAGE_COLS):
                nc = min(STAGE_COLS, cols - c0)
                use = _gate_up_first_use(c0, nc) if cols == 2 * D_FF else c0
                for r0 in range(0, rows, STAGE_ROWS):
                    chunks.append(((w, use, c0, r0), src, dst, r0, min(STAGE_ROWS, rows - r0), c0, nc))
        self.chunks = sorted(chunks, key=lambda c: c[0])
        self.done = 0
        self._copy(0).start()

    def _copy(self, i):
        _, src, _, r0, nr, c0, nc = self.chunks[i]
        return pltpu.make_async_copy(src.at[pl.ds(r0, nr), pl.ds(c0, nc)],
                                     self.stage_ref.at[i % 2, pl.ds(0, nr), pl.ds(0, nc)], self.sem.at[i % 2])

    def _convert_next(self):
        i = self.done
        if i + 1 < len(self.chunks):
            self._copy(i + 1).start()
        self._copy(i).wait()
        _, _, dst, r0, nr, c0, nc = self.chunks[i]
        dst[r0:r0 + nr, c0:c0 + nc] = self.stage_ref[i % 2, 0:nr, 0:nc].astype(BF16)
        self.done += 1

    def need(self, w, rows, cols):
        last = max(i for i, (key, _, _, r0, nr, c0, nc) in enumerate(self.chunks)
                   if key[0] == w and r0 < rows[1] and r0 + nr > rows[0] and c0 < cols[1] and c0 + nc > cols[0])
        while self.done <= last:
            self._convert_next()

    def finish(self):
        while self.done < len(self.chunks):
            self._convert_next()

    def refs(self):
        return tuple(_LazyRef(self, w) for w in range(len(self.vmem_refs)))


class _LazyRef:
    def __init__(self, weights, w):
        self.weights, self.w = weights, w

    def __getitem__(self, idx):
        ref = self.weights.vmem_refs[self.w]
        (r_lo, r_hi, _), (c_lo, c_hi, _) = (sl.indices(size) for sl, size in zip(idx, ref.shape))
        self.weights.need(self.w, (r_lo, r_hi), (c_lo, c_hi))
        return ref[idx]


def _run_staggered(chains, lead):
    chains = list(chains)
    count, done = [0] * len(chains), [False] * len(chains)
    while not all(done):
        for i, chain in enumerate(chains):
            if i > 0 and count[i - 1] < lead[i - 1] and not done[i - 1]:
                break
            if done[i]:
                continue
            try:
                next(chain)
                count[i] += 1
            except StopIteration:
                done[i] = True


def _col_pieces(width):
    return [(lo, lo + COL_PIECE) for lo in range(0, width, COL_PIECE)]


def _ffn_up(x1, gain, wgu_ref, act_ref):
    h = _rms(x1, gain(G_PRE_FFN)).astype(BF16)
    for lo in range(0, D_FF, FF_CHUNK):
        gate = _mm(h, wgu_ref[:, lo:lo + FF_CHUNK])
        up = _mm(h, wgu_ref[:, D_FF + lo:D_FF + lo + FF_CHUNK])
        act_ref[:, lo:lo + FF_CHUNK] = (gate * _sigmoid(gate) * up).astype(BF16)
        yield


def _ffn_down(act_ref, wdown_ref):
    f = []
    for lo, hi in _col_pieces(D_MODEL):
        f.append(_mm(act_ref[...], wdown_ref[:, lo:hi]))
        yield
    return jnp.concatenate(f, axis=-1)


def _embed(x1, f, p, gain, wgate_ref, wproj_ref):
    x2 = x1 + _rms(f, gain(G_POST_FFN))
    hg = _rms(x2, gain(G_PLE)).astype(BF16)
    pb = p.astype(BF16)
    yield
    e = []
    for lo, hi in _col_pieces(D_MODEL):
        e.append(_mm(pb, wproj_ref[:, lo:hi]) * _sigmoid(_mm(hg, wgate_ref[:, lo:hi])))
        yield
    return x2 + _rms(jnp.concatenate(e, axis=-1), gain(G_PLE_POST))


L0_MIX_PIECES = 2 + len(POOL_WINDOWS)
L0_START_B_PIECES = L0_MIX_PIECES + D_FF // FF_CHUNK
L0_CHAIN_PIECES = L0_START_B_PIECES + 2 * (D_MODEL // COL_PIECE) + 1 + 3


def _pool_mixer(x, h_hist, pos0, gain, hext, poolw_ref):
    n = x.shape[0]
    h = _rms(x, gain(G_PRE_MIX))
    hext[0:POOL_HALO, :] = h_hist
    hext[POOL_HALO:POOL_HALO + n, :] = h
    yield
    pos1 = lax.broadcasted_iota(jnp.int32, (n, POOL_GROUP), 0) + (pos0 + 1)
    mixed = []
    for gi, w in enumerate(POOL_WINDOWS):
        lo = gi * POOL_GROUP
        terms = [hext[POOL_HALO - j:POOL_HALO - j + n, lo:lo + POOL_GROUP] for j in range(w)]
        while len(terms) > 1:
            terms = [terms[a] + terms[a + 1] for a in range(0, len(terms), 2)]
        cnt = jnp.minimum(pos1, w).astype(F32)
        d = terms[0] / cnt - h[:, lo:lo + POOL_GROUP]
        mixed.append(_mm(d.astype(BF16), poolw_ref[lo:lo + POOL_GROUP, :]))
        yield
    y = jnp.concatenate(mixed, axis=-1) * gain(G_POOL_SCALE)
    x1 = x + _rms(y, gain(G_POST_MIX))
    yield
    return x1


def _emit_layer0(x3, gain, wkv_ref, xo_ref, kt_ref, v_ref):
    xo_ref[...] = x3
    hkv = _rms(x3, gain(G_KV)).astype(BF16)
    yield
    k = _mm(hkv, wkv_ref[:, :KV_WIDTH])
    for blk in range(x3.shape[0] // ATTN_BLOCK):
        kt_ref[blk] = k[blk * ATTN_BLOCK:(blk + 1) * ATTN_BLOCK, :].T.astype(BF16)
    yield
    v_ref[...] = _mm(hkv, wkv_ref[:, KV_WIDTH:]).astype(BF16)


def _layer0_body(x_ref, xhalo_ref, pa_ref, pb_ref, gains_ref, poolw_hbm, wgu_hbm, wdown_hbm, wgate_hbm,
                 wproj_hbm, wkv_hbm, xoa_ref, kta_ref, va_ref, xob_ref, ktb_ref, vb_ref,
                 hext_ref, acta_ref, actb_ref, x1b_ref, poolw_ref, wgu_ref, wdown_ref, wgate_ref, wproj_ref,
                 wkv_ref, stage_ref, sem, *, tiles_per_seq, n_tiles):
    step = pl.program_id(0)
    gains = gains_ref[...]
    gain = lambda r: gains[r:r + 1, :]
    seq_tile = jnp.minimum(step, n_tiles - 1) % tiles_per_seq
    resident = (poolw_ref, wgu_ref, wdown_ref, wgate_ref, wproj_ref, wkv_ref)

    def finish_b(poolw, wgu, wdown, wgate, wproj, wkv):
        x1 = x1b_ref[...]
        f = yield from _ffn_down(actb_ref, wdown)
        x3 = yield from _embed(x1, f, pb_ref[...], gain, wgate, wproj)
        yield from _emit_layer0(x3, gain, wkv, xob_ref, ktb_ref, vb_ref)

    def chain_a(poolw, wgu, wdown, wgate, wproj, wkv):
        x = x_ref[0:HALF_ROWS, :]
        h_hist = jnp.where(seq_tile == 0, 0.0, _rms(xhalo_ref[...], gain(G_PRE_MIX)))
        x1 = yield from _pool_mixer(x, h_hist, seq_tile * TILE_ROWS, gain, hext_ref.at[0], poolw)
        yield from _ffn_up(x1, gain, wgu, acta_ref)
        f = yield from _ffn_down(acta_ref, wdown)
        x3 = yield from _embed(x1, f, pa_ref[...], gain, wgate, wproj)
        yield from _emit_layer0(x3, gain, wkv, xoa_ref, kta_ref, va_ref)

    def start_b(poolw, wgu, wdown, wgate, wproj, wkv):
        x = x_ref[HALF_ROWS:, :]
        h_hist = _rms(x_ref[HALF_ROWS - POOL_HALO:HALF_ROWS, :], gain(G_PRE_MIX))
        x1 = yield from _pool_mixer(x, h_hist, seq_tile * TILE_ROWS + HALF_ROWS, gain, hext_ref.at[1], poolw)
        x1b_ref[...] = x1
        yield from _ffn_up(x1, gain, wgu, actb_ref)

    lead_b = L0_CHAIN_PIECES - L0_START_B_PIECES

    @pl.when(step == 0)
    def _():
        lazy = _LazyWeights((poolw_hbm.at[0], wgu_hbm.at[0], wdown_hbm.at[0], wgate_hbm.at[0], wproj_hbm.at[0],
                             wkv_hbm), resident, stage_ref, sem)
        _run_staggered([chain_a(*lazy.refs()), start_b(*lazy.refs())], lead=[lead_b])
        lazy.finish()

    @pl.when(jnp.logical_and(step > 0, step < n_tiles))
    def _():
        _run_staggered([finish_b(*resident), chain_a(*resident), start_b(*resident)], lead=[0, lead_b])

    @pl.when(step == n_tiles)
    def _():
        _run_staggered([finish_b(*resident)], lead=[])


def _attention_mixer(x, k_block, v_block, no_prev, gain, bias_ref, attn_ref, wq_ref, wo_ref):
    n = x.shape[0]
    hq = _rms(x, gain(G_PRE_MIX)).astype(BF16)
    yield
    q_pieces = []
    for lo, hi in _col_pieces(D_MODEL):
        q_pieces.append(_mm(hq, wq_ref[:, lo:hi]) * (HEAD_DIM ** -0.5))
        yield
    q = jnp.concatenate(q_pieces, axis=-1)

    left = lax.broadcasted_iota(jnp.int32, (ATTN_BLOCK, LANES), 1) < HEAD_DIM
    first_row = lax.broadcasted_iota(jnp.int32, (ATTN_BLOCK, LANES), 0) == 0
    first_col = lax.broadcasted_iota(jnp.int32, (LANES, ATTN_BLOCK), 1) == 0
    for j in range(n // ATTN_BLOCK):
        r0 = j * ATTN_BLOCK
        for kt in range(KV_TILES):
            kt_win = jnp.concatenate([jnp.where(first_col, 0, k_block(j - 1, kt)), k_block(j, kt)], axis=1)
            v_win = jnp.concatenate([jnp.where(first_row, 0, v_block(j - 1, kt)), v_block(j, kt)], axis=0)
            q_tiles = [q[r0:r0 + ATTN_BLOCK, (kt * GQA_GROUP + g) * LANES:(kt * GQA_GROUP + g + 1) * LANES]
                       for g in range(GQA_GROUP)]
            q8 = jnp.concatenate([jnp.where(left, t, 0.0) for t in q_tiles]
                                 + [jnp.where(left, 0.0, t) for t in q_tiles], axis=0).astype(BF16)
            s = _mm(q8, kt_win)
            s = s + bias_ref[kt]
            s_prev, s_cur = s[:, :ATTN_BLOCK], s[:, ATTN_BLOCK:]
            if j == 0 and no_prev is not None:
                real_key = lax.broadcasted_iota(jnp.int32, (1, LANES), 1) > 0
                s_prev = s_prev + jnp.where(jnp.logical_and(no_prev, real_key), NEG_INF, 0.0)
            m = jnp.max(jnp.maximum(s_prev, s_cur), axis=-1, keepdims=True)
            p_prev, p_cur = jnp.exp(s_prev - m), jnp.exp(s_cur - m)
            denom = jnp.sum(p_prev + p_cur, axis=-1, keepdims=True)
            probs = jnp.concatenate([p_prev, p_cur], axis=-1).astype(BF16)
            o = _mm(probs, v_win) / denom
            for g in range(GQA_GROUP):
                a = o[g * ATTN_BLOCK:(g + 1) * ATTN_BLOCK]
                b = o[(GQA_GROUP + g) * ATTN_BLOCK:(GQA_GROUP + g + 1) * ATTN_BLOCK]
                tile_lo = (kt * GQA_GROUP + g) * LANES
                attn_ref[r0:r0 + ATTN_BLOCK, tile_lo:tile_lo + LANES] = jnp.where(left, a, b).astype(BF16)
            yield

    y = []
    for lo, hi in _col_pieces(D_MODEL):
        y.append(_mm(attn_ref[...], wo_ref[:, lo:hi]))
        yield
    x1 = x + _rms(jnp.concatenate(y, axis=-1), gain(G_POST_MIX))
    yield
    return x1


def _layer1_body(xa_ref, xb_ref, kta_ref, ktb_ref, kthalo_ref, va_ref, vb_ref, vhalo_ref, p_ref, gains_ref, bias_ref,
                 wq_hbm, wo_hbm, wgu_hbm, wdown_hbm, wgate_hbm, wproj_hbm, xo_ref, attn_ref, act_ref, x1c_ref, fc_ref,
                 wq_ref, wo_ref, wgu_ref, wdown_ref, wgate_ref, wproj_ref, stage_ref, sem, *, tiles_per_seq, n_tiles):
    step = pl.program_id(0)
    gains = gains_ref[...]
    gain = lambda r: gains[r:r + 1, :]
    blocks_per_half = HALF_ROWS // ATTN_BLOCK
    resident = (wq_ref, wo_ref, wgu_ref, wdown_ref, wgate_ref, wproj_ref)
    seq_tile = jnp.minimum(step, n_tiles - 1) % tiles_per_seq

    def k_block(blk, kt):
        if blk < 0:
            ref, i = kthalo_ref, 0
        else:
            ref, i = (kta_ref if blk < blocks_per_half else ktb_ref), blk % blocks_per_half
        return ref[i, kt * LANES:(kt + 1) * LANES, :]

    def v_block(blk, kt):
        if blk < 0:
            return vhalo_ref[:, kt * LANES:(kt + 1) * LANES]
        ref = va_ref if blk < blocks_per_half else vb_ref
        r = (blk % blocks_per_half) * ATTN_BLOCK
        return ref[r:r + ATTN_BLOCK, kt * LANES:(kt + 1) * LANES]

    def head():
        x = jnp.concatenate([xa_ref[...], xb_ref[...]], axis=0)
        x1 = yield from _attention_mixer(x, k_block, v_block, seq_tile == 0, gain, bias_ref, attn_ref,
                                         wq_ref, wo_ref)
        x1c_ref[...] = x1
        yield from _ffn_up(x1, gain, wgu_ref, act_ref)
        fc_ref[...] = yield from _ffn_down(act_ref, wdown_ref)

    def tail():
        xo_ref[...] = yield from _embed(x1c_ref[...], fc_ref[...], p_ref[...], gain, wgate_ref, wproj_ref)

    @pl.when(step == 0)
    def _():
        _LazyWeights((wq_hbm, wo_hbm, wgu_hbm.at[1], wdown_hbm.at[1], wgate_hbm.at[1], wproj_hbm.at[1]),
                     resident, stage_ref, sem).finish()
        x1c_ref[...] = jnp.zeros(x1c_ref.shape, F32)
        fc_ref[...] = jnp.zeros(fc_ref.shape, F32)

    @pl.when(step < n_tiles)
    def _():
        h, t = head(), tail()
        n_mixer = 2 + 2 * (D_MODEL // COL_PIECE) + (TILE_ROWS // ATTN_BLOCK) * KV_TILES
        order = [h, t] + [h] * (n_mixer - 2) + [t, h, t, h, t, h, t, h, t]
        for chain in order:
            next(chain, None)
        _run_staggered([h, t], lead=[0])

    @pl.when(step == n_tiles)
    def _():
        _run_staggered([tail()], lead=[])


def _head_order():
    order = []
    for kt in range(KV_TILES):
        for g in range(GQA_GROUP):
            order += [(2 * kt) * GQA_GROUP + g, (2 * kt + 1) * GQA_GROUP + g]
    return order


def _permute_heads(w, axis):
    return jnp.concatenate([lax.slice_in_dim(w, h * HEAD_DIM, (h + 1) * HEAD_DIM, axis=axis)
                            for h in _head_order()], axis=axis)


def _attention_bias(sinks):
    qi = jnp.arange(ATTN_BLOCK)[:, None]
    si = jnp.arange(2 * ATTN_BLOCK)[None, :]
    rel = ATTN_BLOCK + qi - si
    valid = (rel >= 0) & (rel < ATTN_BLOCK)
    heads = jnp.arange(1, N_HEADS + 1, dtype=F32)
    slopes = jnp.exp2(-8.0 * heads / N_HEADS).reshape(N_KV_HEADS, GQA_GROUP)
    bias = jnp.where(valid[None, None], -slopes[:, :, None, None] * rel.astype(F32)[None, None], NEG_INF)
    sink = sinks.astype(F32).reshape(N_KV_HEADS, GQA_GROUP, 1, 1)
    bias = jnp.where(si[None, None] == 0, sink, bias)
    return bias.reshape(KV_TILES, HEADS_PER_KV_TILE * ATTN_BLOCK, 2 * ATTN_BLOCK)


def kernel(x, p, pre_mix_g, post_mix_g, pre_ffn_g, post_ffn_g, pool_w, pool_scale, kv_g, w_kv, w_q, sinks,
           w_o, w_gu, w_down, ple_g, w_ple_gate, w_ple_proj, ple_post_g):
    batch, seq, _ = x.shape
    tm, hm = TILE_ROWS, HALF_ROWS
    assert seq % tm == 0 and hm % ATTN_BLOCK == 0 and hm % POOL_HALO == 0
    tokens = batch * seq
    n_tiles = tokens // tm
    tiles_per_seq = seq // tm

    def gains(i, extra_a, extra_b):
        return jnp.stack([pre_mix_g[i], post_mix_g[i], pre_ffn_g[i], post_ffn_g[i], ple_g[i],
                          ple_post_g[i], extra_a, extra_b]).astype(F32)

    zeros = jnp.zeros((D_MODEL,), F32)
    gains0 = gains(0, pool_scale[0], kv_g)
    gains1 = gains(1, zeros, zeros)
    bias = _attention_bias(sinks[0])
    pool_w_rows = pool_w.reshape(pool_w.shape[0], D_MODEL, POOL_GROUP)
    x_rows = x.reshape(tokens, D_MODEL)
    p_rows = p.reshape(p.shape[0], tokens, PLE_DIM)

    in_hbm = pl.BlockSpec(memory_space=pl.ANY)
    bf16_weight = lambda rows, cols: pltpu.VMEM((rows, cols), BF16)
    ffn_weights = [bf16_weight(D_MODEL, 2 * D_FF), bf16_weight(D_FF, D_MODEL), bf16_weight(D_MODEL, D_MODEL),
                   bf16_weight(PLE_DIM, D_MODEL)]
    staging = [pltpu.VMEM((2, STAGE_ROWS, STAGE_COLS), F32), pltpu.SemaphoreType.DMA((2,))]

    this_tile = lambda s: jnp.minimum(s, n_tiles - 1)
    prev_tile = lambda s: jnp.maximum(s - 1, 0)
    whole = lambda shape: pl.BlockSpec(shape, lambda s: (0,) * len(shape), pipeline_mode=pl.Buffered(1))
    half_a = lambda width: pl.BlockSpec((hm, width), lambda s: (this_tile(s), 0))
    half_b = lambda width: pl.BlockSpec((hm, width), lambda s: (prev_tile(s), 0))
    half_out = lambda width, dtype: jax.ShapeDtypeStruct((tokens // 2, width), dtype)
    blocks_per_half = hm // ATTN_BLOCK
    kt_blocks = lambda n, block: pl.BlockSpec((n, KV_WIDTH, ATTN_BLOCK), lambda s: (block(s), 0, 0))
    kt_out = jax.ShapeDtypeStruct((n_tiles * blocks_per_half, KV_WIDTH, ATTN_BLOCK), BF16)
    x3a, kta, va, x3b, ktb, vb = pl.pallas_call(
        functools.partial(_layer0_body, tiles_per_seq=tiles_per_seq, n_tiles=n_tiles),
        grid=(n_tiles + 1,),
        in_specs=[pl.BlockSpec((tm, D_MODEL), lambda s: (this_tile(s), 0)),
                  pl.BlockSpec((POOL_HALO, D_MODEL),
                               lambda s: (jnp.maximum(this_tile(s) * (tm // POOL_HALO) - 1, 0), 0)),
                  pl.BlockSpec((None, hm, PLE_DIM), lambda s: (0, 2 * this_tile(s), 0)),
                  pl.BlockSpec((None, hm, PLE_DIM), lambda s: (0, 2 * prev_tile(s) + 1, 0)),
                  whole((8, D_MODEL))] + [in_hbm] * 6,
        out_specs=[half_a(D_MODEL), kt_blocks(blocks_per_half, this_tile), half_a(KV_WIDTH),
                   half_b(D_MODEL), kt_blocks(blocks_per_half, prev_tile), half_b(KV_WIDTH)],
        out_shape=[half_out(D_MODEL, F32), kt_out, half_out(KV_WIDTH, BF16)] * 2,
        scratch_shapes=[pltpu.VMEM((2, hm + POOL_HALO, D_MODEL), F32), pltpu.VMEM((hm, D_FF), BF16),
                        pltpu.VMEM((hm, D_FF), BF16), pltpu.VMEM((hm, D_MODEL), F32),
                        bf16_weight(D_MODEL, POOL_GROUP)] + ffn_weights
                       + [bf16_weight(D_MODEL, 2 * KV_WIDTH)] + staging,
        compiler_params=pltpu.CompilerParams(dimension_semantics=("arbitrary",),
                                             vmem_limit_bytes=VMEM_LIMIT_BYTES),
        name="yoco_layer0",
    )(x_rows, x_rows, p_rows, p_rows, gains0, pool_w_rows, w_gu, w_down, w_ple_gate, w_ple_proj, w_kv)

    rows_of = lambda rows, width, block: pl.BlockSpec((rows, width), lambda g: (block(g), 0))
    tile_half = lambda width: rows_of(hm, width, this_tile)
    last_block_before = lambda g: jnp.maximum(this_tile(g) * blocks_per_half - 1, 0)
    kv_before = rows_of(ATTN_BLOCK, KV_WIDTH, last_block_before)
    kt_tile = kt_blocks(blocks_per_half, this_tile)
    carry = pltpu.VMEM((tm, D_MODEL), F32)
    out = pl.pallas_call(
        functools.partial(_layer1_body, tiles_per_seq=tiles_per_seq, n_tiles=n_tiles),
        grid=(n_tiles + 1,),
        in_specs=[tile_half(D_MODEL), tile_half(D_MODEL), kt_tile, kt_tile, kt_blocks(1, last_block_before),
                  tile_half(KV_WIDTH), tile_half(KV_WIDTH), kv_before,
                  pl.BlockSpec((None, tm, PLE_DIM), lambda g: (1, prev_tile(g), 0)), whole((8, D_MODEL)),
                  whole((KV_TILES, HEADS_PER_KV_TILE * ATTN_BLOCK, 2 * ATTN_BLOCK))] + [in_hbm] * 6,
        out_specs=rows_of(tm, D_MODEL, prev_tile),
        out_shape=jax.ShapeDtypeStruct((tokens, D_MODEL), F32),
        scratch_shapes=[pltpu.VMEM((tm, D_MODEL), BF16), pltpu.VMEM((tm, D_FF), BF16), carry, carry,
                        bf16_weight(D_MODEL, D_MODEL), bf16_weight(D_MODEL, D_MODEL)] + ffn_weights + staging,
        compiler_params=pltpu.CompilerParams(dimension_semantics=("arbitrary",),
                                             vmem_limit_bytes=VMEM_LIMIT_BYTES),
        name="yoco_layer1",
    )(x3a, x3b, kta, ktb, ktb, va, vb, vb, p_rows, gains1, bias, _permute_heads(w_q[0], 1), _permute_heads(w_o[0], 0),
      w_gu, w_down, w_ple_gate, w_ple_proj)
    return out.reshape(batch, seq, D_MODEL)
```
